```python
import jax, jax.numpy as jnp
from jax import lax
import numpy as np

D_MODEL = 2048
BATCH = 2
SEQ = 4096
DEPTH = 4

N_BRANCH = 3
NORM_EPS = 1e-6
POOL_WINDOWS = (2, 4, 8, 16)
POOL_GROUPS = 4
POOL_WIDTH = D_MODEL // 2
POOL_GROUP_DIM = POOL_WIDTH // POOL_GROUPS
SSM_INNER = D_MODEL
SSM_HEAD_DIM = 64
SSM_HEADS = SSM_INNER // SSM_HEAD_DIM
SSM_GROUPS = 4
SSM_HEADS_PER_GROUP = SSM_HEADS // SSM_GROUPS
SSM_STATE = 128
SSM_CONV = 4
SSM_CONV_DIM = SSM_INNER + 2 * SSM_GROUPS * SSM_STATE
SSM_CHUNK = 128
SC_WIDTH = D_MODEL // 2
SC_KERNEL = 3
IN_SPLITS = (POOL_WIDTH, POOL_WIDTH, SSM_INNER, SSM_CONV_DIM, SSM_HEADS, SC_WIDTH, SC_WIDTH, SC_WIDTH, SC_WIDTH, N_BRANCH * D_MODEL)
IN_PROJ_DIM = 2 * POOL_WIDTH + SSM_INNER + SSM_CONV_DIM + SSM_HEADS + 4 * SC_WIDTH + N_BRANCH * D_MODEL

kernel_name = 'hybrid_pool_ssd_shortconv_gated_merge'


def rms_norm(x, w):
    xf = x.astype(jnp.float32)
    xf = xf * lax.rsqrt(jnp.mean(xf * xf, axis=-1, keepdims=True) + NORM_EPS)
    return (xf * w.astype(jnp.float32)).astype(x.dtype)


def causal_depthwise_conv(x, w):
    k, c = w.shape
    return lax.conv_general_dilated(x, w[:, None, :].astype(x.dtype), window_strides=(1,), padding=[(k - 1, 0)], dimension_numbers=('NWC', 'WIO', 'NWC'), feature_group_count=c)


def pool_mixer(u, gate, w_grp, scale):
    b, s, _ = u.shape
    cs = jnp.cumsum(u.astype(jnp.float32), axis=1)
    pos = jnp.arange(1, s + 1, dtype=jnp.float32)[None, :, None]
    outs = []
    for g, win in enumerate(POOL_WINDOWS):
        sl = slice(g * POOL_GROUP_DIM, (g + 1) * POOL_GROUP_DIM)
        csg = cs[..., sl]
        lagged = jnp.pad(csg[:, :s - win], ((0, 0), (win, 0), (0, 0)))
        mean = (csg - lagged) / jnp.minimum(pos, win)
        outs.append(mean - u[..., sl].astype(jnp.float32))
    d = jnp.stack(outs, axis=2).astype(u.dtype)
    y = jnp.einsum('bsgc,gcd->bsgd', d, w_grp).reshape(b, s, POOL_WIDTH)
    return y * scale * jax.nn.silu(gate)


def ssd_mixer(xbc, z, dt_raw, conv_w, conv_b, dt_bias, a_log, d_skip, norm_w):
    b, s, _ = xbc.shape
    c, l = s // SSM_CHUNK, SSM_CHUNK
    G, E, P, N = SSM_GROUPS, SSM_HEADS_PER_GROUP, SSM_HEAD_DIM, SSM_STATE
    f32 = jnp.float32
    xbc = jax.nn.silu(causal_depthwise_conv(xbc, conv_w) + conv_b)
    xs, bm, cm = jnp.split(xbc, [SSM_INNER, SSM_INNER + G * N], axis=-1)
    dt = jax.nn.softplus(dt_raw.astype(f32) + dt_bias.astype(f32))
    a = -jnp.exp(a_log.astype(f32))
    xh = xs.reshape(b, c, l, G, E, P).astype(f32)
    bm = bm.reshape(b, c, l, G, N).astype(f32)
    cm = cm.reshape(b, c, l, G, N).astype(f32)
    dtc = dt.reshape(b, c, l, G, E)
    xdt = xh * dtc[..., None]
    log_a = jnp.moveaxis(dtc * a.reshape(G, E), 2, -1)
    a_cum = jnp.cumsum(log_a, axis=-1)
    causal = jnp.tril(jnp.ones((l, l), dtype=bool))
    seg = a_cum[..., :, None] - a_cum[..., None, :]
    decay = jnp.exp(jnp.where(causal, seg, -jnp.inf))
    cb = jnp.einsum('bclgn,bcsgn->bcgls', cm, bm)
    y_diag = jnp.einsum('bcgels,bcsgep->bclgep', cb[:, :, :, None] * decay, xdt)
    to_end = jnp.moveaxis(jnp.exp(a_cum[..., -1:] - a_cum), -1, 2)
    states = jnp.einsum('bclgn,bclgep->bcgepn', bm, xdt * to_end[..., None])
    chunk_decay = jnp.exp(a_cum[..., -1])

    def step(carry, inp):
        st, dec = inp
        return carry * dec[..., None, None] + st, carry

    init = jnp.zeros((b, G, E, P, N), f32)
    _, prev = lax.scan(step, init, (jnp.moveaxis(states, 1, 0), jnp.moveaxis(chunk_decay, 1, 0)))
    prev = jnp.moveaxis(prev, 0, 1)
    from_start = jnp.moveaxis(jnp.exp(a_cum), -1, 2)
    y_off = jnp.einsum('bclgn,bcgepn->bclgep', cm, prev) * from_start[..., None]
    y = y_diag + y_off + xh * d_skip.astype(f32).reshape(G, E, 1)
    y = y.reshape(b, s, SSM_INNER) * jax.nn.silu(z.astype(f32))
    yg = y.reshape(b, s, G, SSM_INNER // G)
    yg = yg * lax.rsqrt(jnp.mean(yg * yg, axis=-1, keepdims=True) + NORM_EPS)
    return (yg.reshape(b, s, SSM_INNER) * norm_w.astype(f32)).astype(z.dtype)


def short_conv_mixer(bg, cg, v, gate, conv_w):
    y = bg * causal_depthwise_conv(cg * v, conv_w)
    return y * jax.nn.silu(gate)


def setup_inputs(seed: int = 0) -> dict:
    key = jax.random.key(seed)
    ks = jax.random.split(key, 20)
    f32 = jnp.float32

    def normal(k, shape, scale):
        return jax.random.normal(k, shape, f32) * scale

    x = normal(ks[0], (BATCH, SEQ, D_MODEL), 1.0)
    norm_w = 1.0 + normal(ks[1], (DEPTH, D_MODEL), 0.02)
    w_in = normal(ks[2], (DEPTH, D_MODEL, IN_PROJ_DIM), D_MODEL ** -0.5)
    b_gate = normal(ks[3], (DEPTH, N_BRANCH * D_MODEL), 0.02)
    pool_w = normal(ks[4], (DEPTH, POOL_GROUPS, POOL_GROUP_DIM, POOL_GROUP_DIM), POOL_GROUP_DIM ** -0.5)
    pool_scale = 1.0 + normal(ks[5], (DEPTH, POOL_WIDTH), 0.02)
    ssm_conv_w = normal(ks[6], (DEPTH, SSM_CONV, SSM_CONV_DIM), SSM_CONV ** -0.5)
    ssm_conv_b = normal(ks[7], (DEPTH, SSM_CONV_DIM), 0.02)
    dt0 = jnp.exp(jax.random.uniform(ks[8], (DEPTH, SSM_HEADS), f32) * (np.log(0.1) - np.log(0.001)) + np.log(0.001))
    ssm_dt_bias = dt0 + jnp.log(-jnp.expm1(-dt0))
    ssm_a_log = jnp.log(jax.random.uniform(ks[9], (DEPTH, SSM_HEADS), f32, minval=1.0, maxval=16.0))
    ssm_d = 1.0 + normal(ks[10], (DEPTH, SSM_HEADS), 0.02)
    ssm_norm_w = 1.0 + normal(ks[11], (DEPTH, SSM_INNER), 0.02)
    sc_conv_w = normal(ks[12], (DEPTH, SC_KERNEL, SC_WIDTH), SC_KERNEL ** -0.5)
    w_br_pool = normal(ks[13], (DEPTH, POOL_WIDTH, D_MODEL), POOL_WIDTH ** -0.5)
    w_br_ssm = normal(ks[14], (DEPTH, SSM_INNER, D_MODEL), SSM_INNER ** -0.5)
    w_br_conv = normal(ks[15], (DEPTH, SC_WIDTH, D_MODEL), SC_WIDTH ** -0.5)
    w_out = normal(ks[16], (DEPTH, D_MODEL, D_MODEL), D_MODEL ** -0.5)
    final_norm_w = 1.0 + normal(ks[17], (D_MODEL,), 0.02)
    return {'x': x, 'norm_w': norm_w, 'w_in': w_in, 'b_gate': b_gate, 'pool_w': pool_w, 'pool_scale': pool_scale, 'ssm_conv_w': ssm_conv_w, 'ssm_conv_b': ssm_conv_b, 'ssm_dt_bias': ssm_dt_bias, 'ssm_a_log': ssm_a_log, 'ssm_d': ssm_d, 'ssm_norm_w': ssm_norm_w, 'sc_conv_w': sc_conv_w, 'w_br_pool': w_br_pool, 'w_br_ssm': w_br_ssm, 'w_br_conv': w_br_conv, 'w_out': w_out, 'final_norm_w': final_norm_w}


def reference(x, norm_w, w_in, b_gate, pool_w, pool_scale, ssm_conv_w, ssm_conv_b, ssm_dt_bias, ssm_a_log, ssm_d, ssm_norm_w, sc_conv_w, w_br_pool, w_br_ssm, w_br_conv, w_out, final_norm_w):
    b, s, _ = x.shape
    offsets = np.cumsum(IN_SPLITS)[:-1].tolist()
    for i in range(DEPTH):
        h = rms_norm(x, norm_w[i])
        proj = h @ w_in[i]
        (p_u, p_g, m_z, m_xbc, m_dt, c_b, c_c, c_v, c_g, g_logit) = jnp.split(proj, offsets, axis=-1)
        gates = jax.nn.sigmoid(g_logit + b_gate[i]).reshape(b, s, N_BRANCH, D_MODEL)
        y_a = pool_mixer(p_u, p_g, pool_w[i], pool_scale[i]) @ w_br_pool[i]
        y_b = ssd_mixer(m_xbc, m_z, m_dt, ssm_conv_w[i], ssm_conv_b[i], ssm_dt_bias[i], ssm_a_log[i], ssm_d[i], ssm_norm_w[i]) @ w_br_ssm[i]
        y_c = short_conv_mixer(c_b, c_c, c_v, c_g, sc_conv_w[i]) @ w_br_conv[i]
        merged = gates[:, :, 0] * y_a + gates[:, :, 1] * y_b + gates[:, :, 2] * y_c
        x = x + merged @ w_out[i]
    return rms_norm(x, final_norm_w)
```

```python
import functools

import jax
import jax.numpy as jnp
from jax import lax
from jax.experimental import pallas as pl
from jax.experimental.pallas import tpu as pltpu

F32 = jnp.float32
BF16 = jnp.bfloat16

NORM_EPS = 1e-6
D_MODEL = 2048
N_BRANCH = 3
POOL_WINDOWS = (2, 4, 8, 16)
POOL_WIDTH = 1024
POOL_GROUP_DIM = 256
SSM_INNER = 2048
SSM_HEAD_DIM = 64
SSM_HEADS = 32
SSM_GROUPS = 4
SSM_STATE = 128
SSM_CONV = 4
SSM_CONV_DIM = SSM_INNER + 2 * SSM_GROUPS * SSM_STATE
CHUNK = 128
SC_WIDTH = 1024
SC_KERNEL = 3

OFF_PU = 0
OFF_PG = 1024
OFF_Z = 2048
OFF_X = 4096
OFF_B = 6144
OFF_C = 6656
OFF_CB = 7168
OFF_CC = 8192
OFF_CV = 9216
OFF_CG = 10240
OFF_GATE = 11264
PROJ_COLS = OFF_GATE + N_BRANCH * D_MODEL
DT_SRC = 7168
DT_PAD = 128

POOL_HALO = 16
CONV_HALO = 8

VMEM_LIMIT = 56 * 1024 * 1024


def _cparams(sem):
    return pltpu.CompilerParams(dimension_semantics=sem, vmem_limit_bytes=VMEM_LIMIT)


def _rmsnorm_body(x_ref, w_ref, o_ref):
    x = x_ref[...]
    ms = jnp.mean(x * x, axis=-1, keepdims=True)
    o_ref[...] = (x * lax.rsqrt(ms + NORM_EPS) * w_ref[...]).astype(o_ref.dtype)


def _rmsnorm(x, w, layer, out_dtype, tm):
    t, d = x.shape
    return pl.pallas_call(
        _rmsnorm_body,
        grid=(t // tm,),
        in_specs=[
            pl.BlockSpec((tm, d), lambda i: (i, 0)),
            pl.BlockSpec((None, 1, d), lambda i: (layer, 0, 0)),
        ],
        out_specs=pl.BlockSpec((tm, d), lambda i: (i, 0)),
        out_shape=jax.ShapeDtypeStruct((t, d), out_dtype),
        compiler_params=_cparams(("arbitrary",)),
        name="rmsnorm",
    )(x, w)


def _in_proj_body(h_ref, w_ref, b_ref, o_ref, *, tn):
    col0 = pl.program_id(0) * tn
    acc = jnp.dot(h_ref[...], w_ref[...], preferred_element_type=F32)
    sig = jax.nn.sigmoid(acc + b_ref[...])
    is_silu = jnp.logical_or(
        jnp.logical_and(col0 >= OFF_PG, col0 < OFF_X),
        jnp.logical_and(col0 >= OFF_CG, col0 < OFF_GATE),
    )
    is_gate = col0 >= OFF_GATE
    out = jnp.where(is_gate, sig, jnp.where(is_silu, acc * sig, acc))
    o_ref[...] = out.astype(o_ref.dtype)


def _in_proj(h, wcat, bias, layer, tm, tn):
    t, k = h.shape
    n = wcat.shape[2]
    return pl.pallas_call(
        functools.partial(_in_proj_body, tn=tn),
        grid=(n // tn, t // tm),
        in_specs=[
            pl.BlockSpec((tm, k), lambda i, j: (j, 0)),
            pl.BlockSpec((None, k, tn), lambda i, j: (layer, 0, i)),
            pl.BlockSpec((None, 1, tn), lambda i, j: (layer, 0, i)),
        ],
        out_specs=pl.BlockSpec((tm, tn), lambda i, j: (j, i)),
        out_shape=jax.ShapeDtypeStruct((t, n), BF16),
        compiler_params=_cparams(("arbitrary", "arbitrary")),
        name="in_proj",
    )(h, wcat, bias)


def _dt_proj_body(h_ref, w_ref, o_ref):
    o_ref[...] = jnp.dot(h_ref[...], w_ref[...], preferred_element_type=F32)


def _dt_proj(h, wdt, layer, tm):
    t, k = h.shape
    n = wdt.shape[2]
    return pl.pallas_call(
        _dt_proj_body,
        grid=(t // tm,),
        in_specs=[
            pl.BlockSpec((tm, k), lambda j: (j, 0)),
            pl.BlockSpec((None, k, n), lambda j: (layer, 0, 0)),
        ],
        out_specs=pl.BlockSpec((tm, n), lambda j: (j, 0)),
        out_shape=jax.ShapeDtypeStruct((t, n), F32),
        compiler_params=_cparams(("arbitrary",)),
        name="dt_proj",
    )(h, wdt)


def _split3(v):
    hi = v.astype(BF16)
    r1 = v - hi.astype(F32)
    mid = r1.astype(BF16)
    lo = (r1 - mid.astype(F32)).astype(BF16)
    return hi, mid, lo


def _dot01_left(m01, v):
    hi, mid, lo = _split3(v)
    return (jnp.dot(m01, hi, preferred_element_type=F32)
            + jnp.dot(m01, mid, preferred_element_type=F32)
            + jnp.dot(m01, lo, preferred_element_type=F32))


def _dot01_right(v, m01):
    hi, mid, lo = _split3(v)
    return (jnp.dot(hi, m01, preferred_element_type=F32)
            + jnp.dot(mid, m01, preferred_element_type=F32)
            + jnp.dot(lo, m01, preferred_element_type=F32))


def _softplus(x):
    return jnp.maximum(x, 0.0) + jnp.log1p(jnp.exp(-jnp.abs(x)))


def _mixer_body(pu_ref, pg_ref, z_ref, x_ref, b_ref, c_ref, dt_ref,
                cbr_ref, ccr_ref, cvr_ref, cgr_ref,
                poolw_ref, pscale_ref, cw_ref, cbias_ref, dtb_ref, alog_ref,
                dexp_ref, nw_ref, scw_ref, e_ref,
                u_ref,
                pool_ext, xbc_ext, cv_ext, state_ref):
    L = CHUNK
    c = pl.program_id(1)

    @pl.when(c == 0)
    def _():
        pool_ext[0:POOL_HALO, :] = jnp.zeros((POOL_HALO, POOL_WIDTH), F32)
        xbc_ext[0:CONV_HALO, :] = jnp.zeros((CONV_HALO, SSM_CONV_DIM), F32)
        cv_ext[0:CONV_HALO, :] = jnp.zeros((CONV_HALO, SC_WIDTH), F32)
        state_ref[...] = jnp.zeros_like(state_ref)

    row = lax.broadcasted_iota(jnp.int32, (L, L), 0)
    col = lax.broadcasted_iota(jnp.int32, (L, L), 1)

    u = pu_ref[...].astype(F32)
    pool_ext[POOL_HALO:POOL_HALO + L, :] = u
    pos = (c * L + 1 + lax.broadcasted_iota(jnp.int32, (L, 1), 0)).astype(F32)
    ys = []
    for g, win in enumerate(POOL_WINDOWS):
        lo, hi = g * POOL_GROUP_DIM, (g + 1) * POOL_GROUP_DIM
        ug = u[:, lo:hi]
        acc = ug
        for j in range(1, win):
            acc = acc + pool_ext[POOL_HALO - j:POOL_HALO - j + L, lo:hi]
        mean = acc / jnp.minimum(pos, float(win))
        d = (mean - ug).astype(BF16)
        ys.append(jnp.dot(d, poolw_ref[g], preferred_element_type=F32))
    ya = jnp.concatenate(ys, axis=1) * pscale_ref[...] * pg_ref[...].astype(F32)
    u_ref[:, SSM_INNER:SSM_INNER + POOL_WIDTH] = ya.astype(BF16)
    pool_ext[0:POOL_HALO, :] = pool_ext[L:L + POOL_HALO, :]

    cvv = ccr_ref[...].astype(F32) * cvr_ref[...].astype(F32)
    cv_ext[CONV_HALO:CONV_HALO + L, :] = cvv
    conv3 = (scw_ref[0:1, :] * cv_ext[CONV_HALO - 2:CONV_HALO - 2 + L, :]
             + scw_ref[1:2, :] * cv_ext[CONV_HALO - 1:CONV_HALO - 1 + L, :]
             + scw_ref[2:3, :] * cvv)
    yc = cbr_ref[...].astype(F32) * conv3 * cgr_ref[...].astype(F32)
    u_ref[:, SSM_INNER + POOL_WIDTH:SSM_INNER + POOL_WIDTH + SC_WIDTH] = yc.astype(BF16)
    cv_ext[0:CONV_HALO, :] = cv_ext[L:L + CONV_HALO, :]

    xbc_ext[CONV_HALO:CONV_HALO + L, 0:SSM_INNER] = x_ref[...].astype(F32)
    xbc_ext[CONV_HALO:CONV_HALO + L, SSM_INNER:SSM_INNER + 512] = b_ref[...].astype(F32)
    xbc_ext[CONV_HALO:CONV_HALO + L, SSM_INNER + 512:SSM_CONV_DIM] = c_ref[...].astype(F32)
    conv = cbias_ref[...]
    for k in range(SSM_CONV):
        s0 = CONV_HALO - (SSM_CONV - 1) + k
        conv = conv + cw_ref[k:k + 1, :] * xbc_ext[s0:s0 + L, :]
    xbc = conv * jax.nn.sigmoid(conv)
    xbc_ext[0:CONV_HALO, :] = xbc_ext[L:L + CONV_HALO, :]
    xs = xbc[:, 0:SSM_INNER]
    bm = xbc[:, SSM_INNER:SSM_INNER + 512]
    cm = xbc[:, SSM_INNER + 512:SSM_CONV_DIM]

    dt = _softplus(dt_ref[...] + dtb_ref[...])
    a = -jnp.exp(alog_ref[...])
    log_a = dt * a
    tril = jnp.where(row >= col, 1.0, 0.0).astype(BF16)
    acum = _dot01_left(tril, log_a)
    acum_t = acum.T
    dt_t = dt.T
    a_last_col = acum_t[:, L - 1:L]
    w_t = jnp.exp(a_last_col - acum_t) * dt_t
    a_last_rows = _dot01_right(acum[L - 16:L, :], e_ref[...])
    cd = jnp.exp(a_last_rows[15:16, :])
    causal = row >= col
    left = col < SSM_HEAD_DIM

    zs = z_ref[...].astype(F32)
    for g in range(SSM_GROUPS):
        n0 = g * SSM_STATE
        c_g = cm[:, n0:n0 + SSM_STATE].astype(BF16)
        bt_g = bm[:, n0:n0 + SSM_STATE].T
        cb_g = jnp.dot(c_g, bt_g.astype(BF16), preferred_element_type=F32)
        g0 = g * 512
        s_g = state_ref[:, g0:g0 + 512]
        yoff_g = jnp.dot(c_g, s_g.astype(BF16), preferred_element_type=F32)
        ygs = []
        for q in range(4):
            j = 4 * g + q
            lane0 = 128 * j
            lhs_parts = []
            acols = []
            for hh in range(2):
                h = 2 * j + hh
                arow = acum_t[h:h + 1, :]
                acol = acum[:, h:h + 1]
                acols.append(acol)
                seg = acol - arow
                decay = jnp.where(causal, jnp.exp(jnp.minimum(seg, 0.0)), 0.0)
                m = (cb_g * decay * dt_t[h:h + 1, :]).astype(BF16)
                btw = (bt_g * w_t[h:h + 1, :]).astype(BF16)
                lhs_parts.append(jnp.concatenate([m, btw], axis=0))
            lhs = jnp.concatenate(lhs_parts, axis=1)
            xp = xs[:, lane0:lane0 + 128]
            rhs = jnp.concatenate(
                [jnp.where(left, xp, 0.0).astype(BF16),
                 jnp.where(left, 0.0, xp).astype(BF16)], axis=0)
            r = jnp.dot(lhs, rhs, preferred_element_type=F32)
            ydiag = r[0:L, :]
            st_new = r[L:2 * L, :]
            fs = jnp.exp(jnp.where(left, acols[0], acols[1]))
            y = (ydiag + yoff_g[:, 128 * q:128 * q + 128] * fs
                 + xp * dexp_ref[:, lane0:lane0 + 128])
            ygs.append(y * zs[:, lane0:lane0 + 128])
            state_ref[:, lane0:lane0 + 128] = (
                s_g[:, 128 * q:128 * q + 128] * cd[:, lane0:lane0 + 128] + st_new)
        yg = jnp.concatenate(ygs, axis=1)
        ms = jnp.mean(yg * yg, axis=1, keepdims=True)
        yb = yg * lax.rsqrt(ms + NORM_EPS) * nw_ref[:, g0:g0 + 512]
        u_ref[:, g0:g0 + 512] = yb.astype(BF16)


def _mixers(proj, dtp, layer, params, batch, seq):
    t = proj.shape[0]
    L = CHUNK
    nc = seq // L
    (poolw, pscale, cw, cbias, dtb, alog, dexp, nw, scw, emat) = params

    def rows(width, off):
        blk = off // width
        return pl.BlockSpec((L, width), lambda b, c: (b * nc + c, blk))

    def lyr(shape):
        nd = len(shape)
        return pl.BlockSpec((None,) + shape, lambda b, c: (layer,) + (0,) * nd)

    in_specs = [
        rows(1024, OFF_PU), rows(1024, OFF_PG), rows(2048, OFF_Z), rows(2048, OFF_X),
        rows(512, OFF_B), rows(512, OFF_C),
        pl.BlockSpec((L, DT_PAD), lambda b, c: (b * nc + c, 0)),
        rows(1024, OFF_CB), rows(1024, OFF_CC), rows(1024, OFF_CV), rows(1024, OFF_CG),
        lyr((SSM_GROUPS, POOL_GROUP_DIM, POOL_GROUP_DIM)),
        lyr((1, POOL_WIDTH)),
        lyr((SSM_CONV, SSM_CONV_DIM)),
        lyr((1, SSM_CONV_DIM)),
        lyr((1, DT_PAD)),
        lyr((1, DT_PAD)),
        lyr((1, SSM_INNER)),
        lyr((1, SSM_INNER)),
        lyr((SC_KERNEL, SC_WIDTH)),
        pl.BlockSpec((DT_PAD, SSM_INNER), lambda b, c: (0, 0)),
    ]
    ucols = SSM_INNER + POOL_WIDTH + SC_WIDTH
    return pl.pallas_call(
        _mixer_body,
        grid=(batch, nc),
        in_specs=in_specs,
        out_specs=pl.BlockSpec((L, ucols), lambda b, c: (b * nc + c, 0)),
        out_shape=jax.ShapeDtypeStruct((t, ucols), BF16),
        scratch_shapes=[
            pltpu.VMEM((POOL_HALO + L, POOL_WIDTH), F32),
            pltpu.VMEM((CONV_HALO + L, SSM_CONV_DIM), F32),
            pltpu.VMEM((CONV_HALO + L, SC_WIDTH), F32),
            pltpu.VMEM((SSM_STATE, SSM_INNER), F32),
        ],
        compiler_params=_cparams(("arbitrary", "arbitrary")),
        name="mixers",
    )(proj, proj, proj, proj, proj, proj, dtp, proj, proj, proj, proj,
      poolw, pscale, cw, cbias, dtb, alog, dexp, nw, scw, emat)


def _merge_body(ub_ref, ua_ref, uc_ref, wa_ref, wb_ref, wc_ref, g0_ref, g1_ref, g2_ref, o_ref):
    ya = jnp.dot(ua_ref[...], wa_ref[...], preferred_element_type=F32)
    yb = jnp.dot(ub_ref[...], wb_ref[...], preferred_element_type=F32)
    yc = jnp.dot(uc_ref[...], wc_ref[...], preferred_element_type=F32)
    acc = (g0_ref[...].astype(F32) * ya + g1_ref[...].astype(F32) * yb
           + g2_ref[...].astype(F32) * yc)
    o_ref[...] = acc.astype(o_ref.dtype)


def _merge(u, proj, wa, wb, wc, layer, tm, tn):
    t = u.shape[0]
    gate_blk = OFF_GATE // tn
    per_gate = D_MODEL // tn
    return pl.pallas_call(
        _merge_body,
        grid=(D_MODEL // tn, t // tm),
        in_specs=[
            pl.BlockSpec((tm, SSM_INNER), lambda i, j: (j, 0)),
            pl.BlockSpec((tm, POOL_WIDTH), lambda i, j: (j, SSM_INNER // POOL_WIDTH)),
            pl.BlockSpec((tm, SC_WIDTH), lambda i, j: (j, (SSM_INNER + POOL_WIDTH) // SC_WIDTH)),
            pl.BlockSpec((None, POOL_WIDTH, tn), lambda i, j: (layer, 0, i)),
            pl.BlockSpec((None, SSM_INNER, tn), lambda i, j: (layer, 0, i)),
            pl.BlockSpec((None, SC_WIDTH, tn), lambda i, j: (layer, 0, i)),
            pl.BlockSpec((tm, tn), lambda i, j: (j, gate_blk + i)),
            pl.BlockSpec((tm, tn), lambda i, j: (j, gate_blk + per_gate + i)),
            pl.BlockSpec((tm, tn), lambda i, j: (j, gate_blk + 2 * per_gate + i)),
        ],
        out_specs=pl.BlockSpec((tm, tn), lambda i, j: (j, i)),
        out_shape=jax.ShapeDtypeStruct((t, D_MODEL), BF16),
        compiler_params=_cparams(("arbitrary", "arbitrary")),
        name="merge",
    )(u, u, u, wa, wb, wc, proj, proj, proj)


def _out_body(m_ref, w_ref, x_ref, nw_ref, *out_refs, emit_x):
    xn = x_ref[...] + jnp.dot(m_ref[...], w_ref[...], preferred_element_type=F32)
    ms = jnp.mean(xn * xn, axis=-1, keepdims=True)
    hn = xn * lax.rsqrt(ms + NORM_EPS) * nw_ref[...]
    if emit_x:
        out_refs[0][...] = xn
        out_refs[1][...] = hn.astype(out_refs[1].dtype)
    else:
        out_refs[0][...] = hn.astype(out_refs[0].dtype)


def _out_proj(merged, wo, x, nw, layer, nw_layer, last, tm):
    t, d = x.shape
    row_spec = pl.BlockSpec((tm, d), lambda j: (j, 0))
    if last:
        out_shape = [jax.ShapeDtypeStruct((t, d), F32)]
        out_specs = [row_spec]
    else:
        out_shape = [jax.ShapeDtypeStruct((t, d), F32), jax.ShapeDtypeStruct((t, d), BF16)]
        out_specs = [row_spec, row_spec]
    return pl.pallas_call(
        functools.partial(_out_body, emit_x=not last),
        grid=(t // tm,),
        in_specs=[
            row_spec,
            pl.BlockSpec((None, d, d), lambda j: (layer, 0, 0)),
            row_spec,
            pl.BlockSpec((None, 1, d), lambda j: (nw_layer, 0, 0)),
        ],
        out_specs=out_specs,
        out_shape=out_shape,
        compiler_params=_cparams(("arbitrary",)),
        name="out_proj",
    )(merged, wo, x, nw)


def kernel(x, norm_w, w_in, b_gate, pool_w, pool_scale, ssm_conv_w, ssm_conv_b, ssm_dt_bias,
           ssm_a_log, ssm_d, ssm_norm_w, sc_conv_w, w_br_pool, w_br_ssm, w_br_conv, w_out,
           final_norm_w):
    batch, seq, d = x.shape
    depth = w_in.shape[0]
    t = batch * seq
    assert d == D_MODEL and seq % CHUNK == 0
    tm = min(1024, t)
    tm_small = min(512, t)
    tn = 1024

    xf = x.reshape(t, d)

    wcat = jnp.concatenate([w_in[:, :, :DT_SRC], w_in[:, :, DT_SRC + SSM_HEADS:]], axis=2).astype(BF16)
    wdt = jnp.pad(w_in[:, :, DT_SRC:DT_SRC + SSM_HEADS],
                  ((0, 0), (0, 0), (0, DT_PAD - SSM_HEADS))).astype(BF16)
    bias = jnp.concatenate([jnp.zeros((depth, OFF_GATE), F32), b_gate.astype(F32)], axis=1)[:, None, :]
    wa = w_br_pool.astype(BF16)
    wb = w_br_ssm.astype(BF16)
    wc = w_br_conv.astype(BF16)
    wo = w_out.astype(BF16)

    pad_h = ((0, 0), (0, DT_PAD - SSM_HEADS))
    head_of_lane = jnp.arange(SSM_INNER, dtype=jnp.int32) // SSM_HEAD_DIM
    emat = (jnp.arange(DT_PAD, dtype=jnp.int32)[:, None] == head_of_lane[None, :]).astype(BF16)
    mix_params = (
        pool_w.astype(BF16),
        pool_scale.astype(F32)[:, None, :],
        ssm_conv_w.astype(F32),
        ssm_conv_b.astype(F32)[:, None, :],
        jnp.pad(ssm_dt_bias.astype(F32), pad_h)[:, None, :],
        jnp.pad(ssm_a_log.astype(F32), pad_h)[:, None, :],
        jnp.repeat(ssm_d.astype(F32), SSM_HEAD_DIM, axis=1)[:, None, :],
        ssm_norm_w.astype(F32)[:, None, :],
        sc_conv_w.astype(F32),
        emat,
    )
    norm_all = jnp.concatenate([norm_w.astype(F32), final_norm_w.astype(F32)[None, :]], axis=0)[:, None, :]

    h = _rmsnorm(xf, norm_all, 0, BF16, tm_small)
    for i in range(depth):
        proj = _in_proj(h, wcat, bias, i, tm, tn)
        dtp = _dt_proj(h, wdt, i, tm)
        u = _mixers(proj, dtp, i, mix_params, batch, seq)
        merged = _merge(u, proj, wa, wb, wc, i, tm_small, tn)
        last = i == depth - 1
        outs = _out_proj(merged, wo, xf, norm_all, i, i + 1, last, tm_small)
        if last:
            return outs[0].reshape(batch, seq, d)
        xf, h = outs
```

```python
import functools

import jax
import jax.numpy as jnp
from jax import lax
from jax.experimental import pallas as pl
from jax.experimental.pallas import tpu as pltpu

F32 = jnp.float32
BF16 = jnp.bfloat16

NORM_EPS = 1e-6
D_MODEL = 2048
N_BRANCH = 3
POOL_WINDOWS = (2, 4, 8, 16)
POOL_WIDTH = 1024
POOL_GROUP_DIM = 256
SSM_INNER = 2048
SSM_HEAD_DIM = 64
SSM_HEADS = 32
SSM_GROUPS = 4
SSM_STATE = 128
SSM_CONV = 4
SSM_CONV_DIM = SSM_INNER + 2 * SSM_GROUPS * SSM_STATE
CHUNK = 128
SC_WIDTH = 1024
SC_KERNEL = 3

OFF_PU = 0
OFF_PG = 1024
OFF_Z = 2048
OFF_X = 4096
OFF_B = 6144
OFF_C = 6656
OFF_CB = 7168
OFF_CC = 8192
OFF_CV = 9216
OFF_CG = 10240
OFF_GATE = 11264
PROJ_COLS = OFF_GATE + N_BRANCH * D_MODEL
LANES = 128
DT_SRC = 7168
DT_PAD = LANES
DT_SHIFT = SSM_HEADS

POOL_HALO = 16
CONV_HALO = 8

VMEM_LIMIT = 56 * 1024 * 1024


def _cparams(sem):
    return pltpu.CompilerParams(dimension_semantics=sem, vmem_limit_bytes=VMEM_LIMIT)


def _rmsnorm_body(x_ref, w_ref, o_ref):
    x = x_ref[...]
    ms = jnp.mean(x * x, axis=-1, keepdims=True)
    o_ref[...] = (x * lax.rsqrt(ms + NORM_EPS) * w_ref[...]).astype(o_ref.dtype)


def _rmsnorm(x, w, layer, out_dtype, tm):
    t, d = x.shape
    return pl.pallas_call(
        _rmsnorm_body,
        grid=(t // tm,),
        in_specs=[
            pl.BlockSpec((tm, d), lambda i: (i, 0)),
            pl.BlockSpec((None, 1, d), lambda i: (layer, 0, 0)),
        ],
        out_specs=pl.BlockSpec((tm, d), lambda i: (i, 0)),
        out_shape=jax.ShapeDtypeStruct((t, d), out_dtype),
        compiler_params=_cparams(("arbitrary",)),
        name="rmsnorm",
    )(x, w)


def _in_proj_body(h_ref, wa_ref, wb_ref, b_ref, o_ref, w_scr, *, tn):
    col0 = pl.program_id(0) * tn
    first_row_tile = pl.program_id(1) == 0
    k = wa_ref.shape[0]
    rc = 256

    @pl.when(jnp.logical_and(first_row_tile, col0 < OFF_CB))
    def _():
        for r in range(k // rc):
            w_scr[r * rc:(r + 1) * rc, :] = wa_ref[r * rc:(r + 1) * rc, :].astype(BF16)

    @pl.when(jnp.logical_and(first_row_tile, col0 >= OFF_CB))
    def _():
        for r in range(k // rc):
            cat = jnp.concatenate(
                [wa_ref[r * rc:(r + 1) * rc, :], wb_ref[r * rc:(r + 1) * rc, :]], axis=1)
            rolled = pltpu.roll(cat, shift=tn + LANES - DT_SHIFT, axis=1)
            w_scr[r * rc:(r + 1) * rc, :] = rolled[:, 0:tn].astype(BF16)

    acc = jnp.dot(h_ref[...], w_scr[...], preferred_element_type=F32)
    sig = jax.nn.sigmoid(acc + b_ref[...])
    is_silu = jnp.logical_or(
        jnp.logical_and(col0 >= OFF_PG, col0 < OFF_X),
        jnp.logical_and(col0 >= OFF_CG, col0 < OFF_GATE),
    )
    is_gate = col0 >= OFF_GATE
    out = jnp.where(is_gate, sig, jnp.where(is_silu, acc * sig, acc))
    o_ref[...] = out.astype(o_ref.dtype)


def _in_proj(h, w_in, bias, layer, tm, tn):
    t, k = h.shape
    assert OFF_CB % tn == 0 and tn % LANES == 0
    lane_blocks = tn // LANES
    return pl.pallas_call(
        functools.partial(_in_proj_body, tn=tn),
        grid=(PROJ_COLS // tn, t // tm),
        in_specs=[
            pl.BlockSpec((tm, k), lambda i, j: (j, 0)),
            pl.BlockSpec((None, k, tn), lambda i, j: (layer, 0, i)),
            pl.BlockSpec((None, k, LANES), lambda i, j: (layer, 0, (i + 1) * lane_blocks)),
            pl.BlockSpec((None, 1, tn), lambda i, j: (layer, 0, i)),
        ],
        out_specs=pl.BlockSpec((tm, tn), lambda i, j: (j, i)),
        out_shape=jax.ShapeDtypeStruct((t, PROJ_COLS), BF16),
        scratch_shapes=[pltpu.VMEM((k, tn), BF16)],
        compiler_params=_cparams(("arbitrary", "arbitrary")),
        name="in_proj",
    )(h, w_in, w_in, bias)


def _dt_proj_body(h_ref, w_ref, o_ref):
    o_ref[...] = jnp.dot(h_ref[...], w_ref[...].astype(BF16), preferred_element_type=F32)


def _dt_proj(h, w_in, layer, tm):
    t, k = h.shape
    return pl.pallas_call(
        _dt_proj_body,
        grid=(t // tm,),
        in_specs=[
            pl.BlockSpec((tm, k), lambda j: (j, 0)),
            pl.BlockSpec((None, k, DT_PAD), lambda j: (layer, 0, DT_SRC // DT_PAD)),
        ],
        out_specs=pl.BlockSpec((tm, DT_PAD), lambda j: (j, 0)),
        out_shape=jax.ShapeDtypeStruct((t, DT_PAD), F32),
        compiler_params=_cparams(("arbitrary",)),
        name="dt_proj",
    )(h, w_in)


def _split3(v):
    hi = v.astype(BF16)
    r1 = v - hi.astype(F32)
    mid = r1.astype(BF16)
    lo = (r1 - mid.astype(F32)).astype(BF16)
    return hi, mid, lo


def _dot01_left(m01, v):
    hi, mid, lo = _split3(v)
    return (jnp.dot(m01, hi, preferred_element_type=F32)
            + jnp.dot(m01, mid, preferred_element_type=F32)
            + jnp.dot(m01, lo, preferred_element_type=F32))


def _dot01_right(v, m01):
    hi, mid, lo = _split3(v)
    return (jnp.dot(hi, m01, preferred_element_type=F32)
            + jnp.dot(mid, m01, preferred_element_type=F32)
            + jnp.dot(lo, m01, preferred_element_type=F32))


def _softplus(x):
    return jnp.maximum(x, 0.0) + jnp.log1p(jnp.exp(-jnp.abs(x)))


def _mixer_body(pu_ref, pg_ref, z_ref, x_ref, b_ref, c_ref, dt_ref,
                cbr_ref, ccr_ref, cvr_ref, cgr_ref,
                poolw_ref, pscale_ref, cw_ref, cbias_ref, dtb_ref, alog_ref,
                dexp_ref, nw_ref, scw_ref, e_ref,
                u_ref,
                pool_ext, xbc_ext, cv_ext, state_ref):
    L = CHUNK
    c = pl.program_id(1)

    @pl.when(c == 0)
    def _():
        pool_ext[0:POOL_HALO, :] = jnp.zeros((POOL_HALO, POOL_WIDTH), F32)
        xbc_ext[0:CONV_HALO, :] = jnp.zeros((CONV_HALO, SSM_CONV_DIM), F32)
        cv_ext[0:CONV_HALO, :] = jnp.zeros((CONV_HALO, SC_WIDTH), F32)
        state_ref[...] = jnp.zeros_like(state_ref)

    row = lax.broadcasted_iota(jnp.int32, (L, L), 0)
    col = lax.broadcasted_iota(jnp.int32, (L, L), 1)

    u = pu_ref[...].astype(F32)
    pool_ext[POOL_HALO:POOL_HALO + L, :] = u
    pos = (c * L + 1 + lax.broadcasted_iota(jnp.int32, (L, 1), 0)).astype(F32)
    ys = []
    for g, win in enumerate(POOL_WINDOWS):
        lo, hi = g * POOL_GROUP_DIM, (g + 1) * POOL_GROUP_DIM
        ug = u[:, lo:hi]
        acc = ug
        for j in range(1, win):
            acc = acc + pool_ext[POOL_HALO - j:POOL_HALO - j + L, lo:hi]
        mean = acc / jnp.minimum(pos, float(win))
        d = (mean - ug).astype(BF16)
        ys.append(jnp.dot(d, poolw_ref[g], preferred_element_type=F32))
    ya = jnp.concatenate(ys, axis=1) * pscale_ref[...] * pg_ref[...].astype(F32)
    u_ref[:, SSM_INNER:SSM_INNER + POOL_WIDTH] = ya.astype(BF16)
    pool_ext[0:POOL_HALO, :] = pool_ext[L:L + POOL_HALO, :]

    cvv = ccr_ref[...].astype(F32) * cvr_ref[...].astype(F32)
    cv_ext[CONV_HALO:CONV_HALO + L, :] = cvv
    conv3 = (scw_ref[0:1, :] * cv_ext[CONV_HALO - 2:CONV_HALO - 2 + L, :]
             + scw_ref[1:2, :] * cv_ext[CONV_HALO - 1:CONV_HALO - 1 + L, :]
             + scw_ref[2:3, :] * cvv)
    yc = cbr_ref[...].astype(F32) * conv3 * cgr_ref[...].astype(F32)
    u_ref[:, SSM_INNER + POOL_WIDTH:SSM_INNER + POOL_WIDTH + SC_WIDTH] = yc.astype(BF16)
    cv_ext[0:CONV_HALO, :] = cv_ext[L:L + CONV_HALO, :]

    xbc_ext[CONV_HALO:CONV_HALO + L, 0:SSM_INNER] = x_ref[...].astype(F32)
    xbc_ext[CONV_HALO:CONV_HALO + L, SSM_INNER:SSM_INNER + 512] = b_ref[...].astype(F32)
    xbc_ext[CONV_HALO:CONV_HALO + L, SSM_INNER + 512:SSM_CONV_DIM] = c_ref[...].astype(F32)
    conv = cbias_ref[...]
    for k in range(SSM_CONV):
        s0 = CONV_HALO - (SSM_CONV - 1) + k
        conv = conv + cw_ref[k:k + 1, :] * xbc_ext[s0:s0 + L, :]
    xbc = conv * jax.nn.sigmoid(conv)
    xbc_ext[0:CONV_HALO, :] = xbc_ext[L:L + CONV_HALO, :]
    xs = xbc[:, 0:SSM_INNER]
    bm = xbc[:, SSM_INNER:SSM_INNER + 512]
    cm = xbc[:, SSM_INNER + 512:SSM_CONV_DIM]

    dt = _softplus(dt_ref[...] + dtb_ref[...])
    a = -jnp.exp(alog_ref[...])
    log_a = dt * a
    tril = jnp.where(row >= col, 1.0, 0.0).astype(BF16)
    acum = _dot01_left(tril, log_a)
    acum_t = acum.T
    dt_t = dt.T
    a_last_col = acum_t[:, L - 1:L]
    w_t = jnp.exp(a_last_col - acum_t) * dt_t
    a_last_rows = _dot01_right(acum[L - 16:L, :], e_ref[...])
    cd = jnp.exp(a_last_rows[15:16, :])
    causal = row >= col
    left = col < SSM_HEAD_DIM

    zs = z_ref[...].astype(F32)
    for g in range(SSM_GROUPS):
        n0 = g * SSM_STATE
        c_g = cm[:, n0:n0 + SSM_STATE].astype(BF16)
        bt_g = bm[:, n0:n0 + SSM_STATE].T
        cb_g = jnp.dot(c_g, bt_g.astype(BF16), preferred_element_type=F32)
        g0 = g * 512
        s_g = state_ref[:, g0:g0 + 512]
        yoff_g = jnp.dot(c_g, s_g.astype(BF16), preferred_element_type=F32)
        ygs = []
        for q in range(4):
            j = 4 * g + q
            lane0 = 128 * j
            lhs_parts = []
            acols = []
            for hh in range(2):
                h = 2 * j + hh
                arow = acum_t[h:h + 1, :]
                acol = acum[:, h:h + 1]
                acols.append(acol)
                seg = acol - arow
                decay = jnp.where(causal, jnp.exp(jnp.minimum(seg, 0.0)), 0.0)
                m = (cb_g * decay * dt_t[h:h + 1, :]).astype(BF16)
                btw = (bt_g * w_t[h:h + 1, :]).astype(BF16)
                lhs_parts.append(jnp.concatenate([m, btw], axis=0))
            lhs = jnp.concatenate(lhs_parts, axis=1)
            xp = xs[:, lane0:lane0 + 128]
            rhs = jnp.concatenate(
                [jnp.where(left, xp, 0.0).astype(BF16),
                 jnp.where(left, 0.0, xp).astype(BF16)], axis=0)
            r = jnp.dot(lhs, rhs, preferred_element_type=F32)
            ydiag = r[0:L, :]
            st_new = r[L:2 * L, :]
            fs = jnp.exp(jnp.where(left, acols[0], acols[1]))
            y = (ydiag + yoff_g[:, 128 * q:128 * q + 128] * fs
                 + xp * dexp_ref[:, lane0:lane0 + 128])
            ygs.append(y * zs[:, lane0:lane0 + 128])
            state_ref[:, lane0:lane0 + 128] = (
                s_g[:, 128 * q:128 * q + 128] * cd[:, lane0:lane0 + 128] + st_new)
        yg = jnp.concatenate(ygs, axis=1)
        ms = jnp.mean(yg * yg, axis=1, keepdims=True)
        yb = yg * lax.rsqrt(ms + NORM_EPS) * nw_ref[:, g0:g0 + 512]
        u_ref[:, g0:g0 + 512] = yb.astype(BF16)


def _mixers(proj, dtp, layer, params, batch, seq):
    t = proj.shape[0]
    L = CHUNK
    nc = seq // L
    (poolw, pscale, cw, cbias, dtb, alog, dexp, nw, scw, emat) = params

    def rows(width, off):
        blk = off // width
        return pl.BlockSpec((L, width), lambda b, c: (b * nc + c, blk))

    def lyr(shape):
        nd = len(shape)
        return pl.BlockSpec((None,) + shape, lambda b, c: (layer,) + (0,) * nd)

    in_specs = [
        rows(1024, OFF_PU), rows(1024, OFF_PG), rows(2048, OFF_Z), rows(2048, OFF_X),
        rows(512, OFF_B), rows(512, OFF_C),
        pl.BlockSpec((L, DT_PAD), lambda b, c: (b * nc + c, 0)),
        rows(1024, OFF_CB), rows(1024, OFF_CC), rows(1024, OFF_CV), rows(1024, OFF_CG),
        lyr((SSM_GROUPS, POOL_GROUP_DIM, POOL_GROUP_DIM)),
        lyr((1, POOL_WIDTH)),
        lyr((SSM_CONV, SSM_CONV_DIM)),
        lyr((1, SSM_CONV_DIM)),
        lyr((1, DT_PAD)),
        lyr((1, DT_PAD)),
        lyr((1, SSM_INNER)),
        lyr((1, SSM_INNER)),
        lyr((SC_KERNEL, SC_WIDTH)),
        pl.BlockSpec((DT_PAD, SSM_INNER), lambda b, c: (0, 0)),
    ]
    ucols = SSM_INNER + POOL_WIDTH + SC_WIDTH
    return pl.pallas_call(
        _mixer_body,
        grid=(batch, nc),
        in_specs=in_specs,
        out_specs=pl.BlockSpec((L, ucols), lambda b, c: (b * nc + c, 0)),
        out_shape=jax.ShapeDtypeStruct((t, ucols), BF16),
        scratch_shapes=[
            pltpu.VMEM((POOL_HALO + L, POOL_WIDTH), F32),
            pltpu.VMEM((CONV_HALO + L, SSM_CONV_DIM), F32),
            pltpu.VMEM((CONV_HALO + L, SC_WIDTH), F32),
            pltpu.VMEM((SSM_STATE, SSM_INNER), F32),
        ],
        compiler_params=_cparams(("arbitrary", "arbitrary")),
        name="mixers",
    )(proj, proj, proj, proj, proj, proj, dtp, proj, proj, proj, proj,
      poolw, pscale, cw, cbias, dtb, alog, dexp, nw, scw, emat)


def _merge_body(ub_ref, ua_ref, uc_ref, wa_ref, wb_ref, wc_ref, g0_ref, g1_ref, g2_ref, o_ref):
    ya = jnp.dot(ua_ref[...], wa_ref[...], preferred_element_type=F32)
    yb = jnp.dot(ub_ref[...], wb_ref[...], preferred_element_type=F32)
    yc = jnp.dot(uc_ref[...], wc_ref[...], preferred_element_type=F32)
    acc = (g0_ref[...].astype(F32) * ya + g1_ref[...].astype(F32) * yb
           + g2_ref[...].astype(F32) * yc)
    o_ref[...] = acc.astype(o_ref.dtype)


def _merge(u, proj, wa, wb, wc, layer, tm, tn):
    t = u.shape[0]
    gate_blk = OFF_GATE // tn
    per_gate = D_MODEL // tn
    return pl.pallas_call(
        _merge_body,
        grid=(D_MODEL // tn, t // tm),
        in_specs=[
            pl.BlockSpec((tm, SSM_INNER), lambda i, j: (j, 0)),
            pl.BlockSpec((tm, POOL_WIDTH), lambda i, j: (j, SSM_INNER // POOL_WIDTH)),
            pl.BlockSpec((tm, SC_WIDTH), lambda i, j: (j, (SSM_INNER + POOL_WIDTH) // SC_WIDTH)),
            pl.BlockSpec((None, POOL_WIDTH, tn), lambda i, j: (layer, 0, i)),
            pl.BlockSpec((None, SSM_INNER, tn), lambda i, j: (layer, 0, i)),
            pl.BlockSpec((None, SC_WIDTH, tn), lambda i, j: (layer, 0, i)),
            pl.BlockSpec((tm, tn), lambda i, j: (j, gate_blk + i)),
            pl.BlockSpec((tm, tn), lambda i, j: (j, gate_blk + per_gate + i)),
            pl.BlockSpec((tm, tn), lambda i, j: (j, gate_blk + 2 * per_gate + i)),
        ],
        out_specs=pl.BlockSpec((tm, tn), lambda i, j: (j, i)),
        out_shape=jax.ShapeDtypeStruct((t, D_MODEL), BF16),
        compiler_params=_cparams(("arbitrary", "arbitrary")),
        name="merge",
    )(u, u, u, wa, wb, wc, proj, proj, proj)


def _out_body(m_ref, w_ref, x_ref, nw_ref, *out_refs, emit_x):
    xn = x_ref[...] + jnp.dot(m_ref[...], w_ref[...], preferred_element_type=F32)
    ms = jnp.mean(xn * xn, axis=-1, keepdims=True)
    hn = xn * lax.rsqrt(ms + NORM_EPS) * nw_ref[...]
    if emit_x:
        out_refs[0][...] = xn
        out_refs[1][...] = hn.astype(out_refs[1].dtype)
    else:
        out_refs[0][...] = hn.astype(out_refs[0].dtype)


def _out_proj(merged, wo, x, nw, layer, nw_layer, last, tm):
    t, d = x.shape
    row_spec = pl.BlockSpec((tm, d), lambda j: (j, 0))
    if last:
        out_shape = [jax.ShapeDtypeStruct((t, d), F32)]
        out_specs = [row_spec]
    else:
        out_shape = [jax.ShapeDtypeStruct((t, d), F32), jax.ShapeDtypeStruct((t, d), BF16)]
        out_specs = [row_spec, row_spec]
    return pl.pallas_call(
        functools.partial(_out_body, emit_x=not last),
        grid=(t // tm,),
        in_specs=[
            row_spec,
            pl.BlockSpec((None, d, d), lambda j: (layer, 0, 0)),
            row_spec,
            pl.BlockSpec((None, 1, d), lambda j: (nw_layer, 0, 0)),
        ],
        out_specs=out_specs,
        out_shape=out_shape,
        compiler_params=_cparams(("arbitrary",)),
        name="out_proj",
    )(merged, wo, x, nw)


def kernel(x, norm_w, w_in, b_gate, pool_w, pool_scale, ssm_conv_w, ssm_conv_b, ssm_dt_bias,
           ssm_a_log, ssm_d, ssm_norm_w, sc_conv_w, w_br_pool, w_br_ssm, w_br_conv, w_out,
           final_norm_w):
    batch, seq, d = x.shape
    depth = w_in.shape[0]
    t = batch * seq
    assert d == D_MODEL and seq % CHUNK == 0
    tm = min(1024, t)
    tm_small = min(512, t)
    tn = 1024

    xf = x.reshape(t, d)

    w_in = w_in.astype(F32)
    bias = jnp.concatenate([jnp.zeros((depth, OFF_GATE), F32), b_gate.astype(F32)], axis=1)[:, None, :]
    wa = w_br_pool.astype(BF16)
    wb = w_br_ssm.astype(BF16)
    wc = w_br_conv.astype(BF16)
    wo = w_out.astype(BF16)

    pad_h = ((0, 0), (0, DT_PAD - SSM_HEADS))
    head_of_lane = jnp.arange(SSM_INNER, dtype=jnp.int32) // SSM_HEAD_DIM
    emat = (jnp.arange(DT_PAD, dtype=jnp.int32)[:, None] == head_of_lane[None, :]).astype(BF16)
    mix_params = (
        pool_w.astype(BF16),
        pool_scale.astype(F32)[:, None, :],
        ssm_conv_w.astype(F32),
        ssm_conv_b.astype(F32)[:, None, :],
        jnp.pad(ssm_dt_bias.astype(F32), pad_h)[:, None, :],
        jnp.pad(ssm_a_log.astype(F32), pad_h)[:, None, :],
        jnp.repeat(ssm_d.astype(F32), SSM_HEAD_DIM, axis=1)[:, None, :],
        ssm_norm_w.astype(F32)[:, None, :],
        sc_conv_w.astype(F32),
        emat,
    )
    norm_all = jnp.concatenate([norm_w.astype(F32), final_norm_w.astype(F32)[None, :]], axis=0)[:, None, :]

    h = _rmsnorm(xf, norm_all, 0, BF16, tm_small)
    for i in range(depth):
        proj = _in_proj(h, w_in, bias, i, tm, tn)
        dtp = _dt_proj(h, w_in, i, tm)
        u = _mixers(proj, dtp, i, mix_params, batch, seq)
        merged = _merge(u, proj, wa, wb, wc, i, tm_small, tn)
        last = i == depth - 1
        outs = _out_proj(merged, wo, xf, norm_all, i, i + 1, last, tm_small)
        if last:
            return outs[0].reshape(batch, seq, d)
        xf, h = outs
```

```python
import functools

import jax
import jax.numpy as jnp
from jax import lax
from jax.experimental import pallas as pl
from jax.experimental.pallas import tpu as pltpu

F32 = jnp.float32
BF16 = jnp.bfloat16

NORM_EPS = 1e-6
D_MODEL = 2048
N_BRANCH = 3
POOL_WINDOWS = (2, 4, 8, 16)
POOL_WIDTH = 1024
POOL_GROUP_DIM = 256
SSM_INNER = 2048
SSM_HEAD_DIM = 64
SSM_HEADS = 32
SSM_GROUPS = 4
SSM_STATE = 128
SSM_CONV = 4
SSM_CONV_DIM = SSM_INNER + 2 * SSM_GROUPS * SSM_STATE
CHUNK = 128
SC_WIDTH = 1024
SC_KERNEL = 3

OFF_PU = 0
OFF_PG = 1024
OFF_Z = 2048
OFF_X = 4096
OFF_B = 6144
OFF_C = 6656
OFF_CB = 7168
OFF_CC = 8192
OFF_CV = 9216
OFF_CG = 10240
OFF_GATE = 11264
PROJ_COLS = OFF_GATE + N_BRANCH * D_MODEL
LANES = 128
DT_SRC = 7168
DT_PAD = LANES
DT_SHIFT = SSM_HEADS

CONV_HALO = 8
BF16_ROWS = 16
LOG2E = 1.4426950408889634
MASK_EXPONENT = -1e30

VMEM_LIMIT = 56 * 1024 * 1024


def _cparams(sem):
    return pltpu.CompilerParams(dimension_semantics=sem, vmem_limit_bytes=VMEM_LIMIT)


def _cast_rows(dst_ref, src_ref, rows_per_chunk=256):
    n = src_ref.shape[0]
    for r in range(0, n, rows_per_chunk):
        dst_ref[r:r + rows_per_chunk, :] = src_ref[r:r + rows_per_chunk, :].astype(dst_ref.dtype)


def _rmsnorm_body(x_ref, w_ref, o_ref):
    x = x_ref[...]
    ms = jnp.mean(x * x, axis=-1, keepdims=True)
    o_ref[...] = (x * lax.rsqrt(ms + NORM_EPS) * w_ref[...]).astype(o_ref.dtype)


def _rmsnorm(x, w, layer, out_dtype, tm):
    t, d = x.shape
    return pl.pallas_call(
        _rmsnorm_body,
        grid=(t // tm,),
        in_specs=[
            pl.BlockSpec((tm, d), lambda i: (i, 0)),
            pl.BlockSpec((None, 1, d), lambda i: (layer, 0, 0)),
        ],
        out_specs=pl.BlockSpec((tm, d), lambda i: (i, 0)),
        out_shape=jax.ShapeDtypeStruct((t, d), out_dtype),
        compiler_params=_cparams(("arbitrary",)),
        name="rmsnorm",
    )(x, w)


_NT = (((1,), (1,)), ((), ()))


def _in_proj_body(h_ref, w_ref, b_ref, o_ref, w_scr, *, tn):
    col0 = pl.program_id(0) * tn

    @pl.when(pl.program_id(1) == 0)
    def _():
        _cast_rows(w_scr, w_ref)

    is_silu = jnp.logical_or(
        jnp.logical_and(col0 >= OFF_PG, col0 < OFF_X),
        jnp.logical_and(col0 >= OFF_CG, col0 < OFF_GATE),
    )
    is_gate = col0 >= OFF_GATE

    def tile(kind):
        acc = lax.dot_general(h_ref[...], w_scr[...], _NT, preferred_element_type=F32)
        if kind == "gate":
            out = 0.5 * jnp.tanh(0.5 * (acc + b_ref[...])) + 0.5
        elif kind == "silu":
            half = 0.5 * acc
            out = half * jnp.tanh(half) + half
        else:
            out = acc
        o_ref[...] = out.astype(o_ref.dtype)

    pl.when(is_gate)(functools.partial(tile, "gate"))
    pl.when(is_silu)(functools.partial(tile, "silu"))
    pl.when(jnp.logical_not(jnp.logical_or(is_gate, is_silu)))(functools.partial(tile, "plain"))


def _in_proj(h, w_in_t, bias, layer, tm, tn):
    t, k = h.shape
    assert OFF_CB % tn == 0

    depth, n_src, _ = w_in_t.shape
    w_rows_flat = w_in_t.reshape(depth * n_src, k)

    def w_rows(i, j):
        start = i * tn
        row = layer * n_src + start + jnp.where(start >= OFF_CB, DT_SHIFT, 0)
        return (pl.multiple_of(row, DT_SHIFT), 0)

    return pl.pallas_call(
        functools.partial(_in_proj_body, tn=tn),
        grid=(PROJ_COLS // tn, t // tm),
        in_specs=[
            pl.BlockSpec((tm, k), lambda i, j: (j, 0)),
            pl.BlockSpec((pl.Element(tn), pl.Element(k)), w_rows),
            pl.BlockSpec((None, 1, tn), lambda i, j: (layer, 0, i)),
        ],
        out_specs=pl.BlockSpec((tm, tn), lambda i, j: (j, i)),
        out_shape=jax.ShapeDtypeStruct((t, PROJ_COLS), BF16),
        scratch_shapes=[pltpu.VMEM((tn, k), BF16)],
        compiler_params=_cparams(("arbitrary", "arbitrary")),
        name="in_proj",
    )(h, w_rows_flat, bias)


def _dt_proj_body(h_ref, w_ref, o_ref):
    o_ref[...] = lax.dot_general(h_ref[...], w_ref[...].astype(BF16), _NT,
                                 preferred_element_type=F32)


def _dt_proj(h, w_in_t, layer, tm):
    t, k = h.shape
    return pl.pallas_call(
        _dt_proj_body,
        grid=(t // tm,),
        in_specs=[
            pl.BlockSpec((tm, k), lambda j: (j, 0)),
            pl.BlockSpec((None, DT_PAD, k), lambda j: (layer, DT_SRC // DT_PAD, 0)),
        ],
        out_specs=pl.BlockSpec((tm, DT_PAD), lambda j: (j, 0)),
        out_shape=jax.ShapeDtypeStruct((t, DT_PAD), F32),
        compiler_params=_cparams(("arbitrary",)),
        name="dt_proj",
    )(h, w_in_t)


def _split3(v):
    hi = v.astype(BF16)
    r1 = v - hi.astype(F32)
    mid = r1.astype(BF16)
    lo = (r1 - mid.astype(F32)).astype(BF16)
    return hi, mid, lo


def _dot01_left(m01, v):
    hi, mid, lo = _split3(v)
    return (jnp.dot(m01, hi, preferred_element_type=F32)
            + jnp.dot(m01, mid, preferred_element_type=F32)
            + jnp.dot(m01, lo, preferred_element_type=F32))


def _dot01_right(v, m01):
    hi, mid, lo = _split3(v)
    return (jnp.dot(hi, m01, preferred_element_type=F32)
            + jnp.dot(mid, m01, preferred_element_type=F32)
            + jnp.dot(lo, m01, preferred_element_type=F32))


def _softplus(x):
    return jnp.maximum(x, 0.0) + jnp.log1p(jnp.exp(-jnp.abs(x)))


def _mixer_body(pu_ref, pg_ref, z_ref, x_ref, b_ref, c_ref, dt_ref,
                cbr_ref, ccr_ref, cvr_ref, cgr_ref,
                poolw_ref, pscale_ref, cw_ref, cbias_ref, dtb_ref, alog_ref,
                dexp_ref, nw_ref, scw_ref, e_ref, band_ref, shift_ref,
                u_ref,
                pool_ext, xbc_ext, cv_ext, state_ref):
    L = CHUNK
    c = pl.program_id(1)

    @pl.when(c == 0)
    def _():
        pool_ext[0:L, :] = jnp.zeros((L, POOL_WIDTH), BF16)
        xbc_ext[0:L, :] = jnp.zeros((L, SSM_CONV_DIM), BF16)
        cv_ext[0:CONV_HALO, :] = jnp.zeros((CONV_HALO, SC_WIDTH), F32)
        state_ref[...] = jnp.zeros_like(state_ref)

    row = lax.broadcasted_iota(jnp.int32, (L, L), 0)
    col = lax.broadcasted_iota(jnp.int32, (L, L), 1)

    pool_ext[L:2 * L, :] = pu_ref[...]
    u = pu_ref[...].astype(F32)
    pos = (c * L + 1 + lax.broadcasted_iota(jnp.int32, (L, 1), 0)).astype(F32)
    ys = []
    for g, win in enumerate(POOL_WINDOWS):
        lo, hi = g * POOL_GROUP_DIM, (g + 1) * POOL_GROUP_DIM
        wsum = jnp.dot(band_ref[g], pool_ext[:, lo:hi], preferred_element_type=F32)
        mean = wsum / jnp.minimum(pos, float(win))
        d = (mean - u[:, lo:hi]).astype(BF16)
        ys.append(jnp.dot(d, poolw_ref[g], preferred_element_type=F32))
    ya = jnp.concatenate(ys, axis=1) * pscale_ref[...] * pg_ref[...].astype(F32)
    u_ref[:, SSM_INNER:SSM_INNER + POOL_WIDTH] = ya.astype(BF16)
    pool_ext[L - BF16_ROWS:L, :] = pool_ext[2 * L - BF16_ROWS:2 * L, :]

    cvv = ccr_ref[...].astype(F32) * cvr_ref[...].astype(F32)
    cv_ext[CONV_HALO:CONV_HALO + L, :] = cvv
    conv3 = (scw_ref[0:1, :] * cv_ext[CONV_HALO - 2:CONV_HALO - 2 + L, :]
             + scw_ref[1:2, :] * cv_ext[CONV_HALO - 1:CONV_HALO - 1 + L, :]
             + scw_ref[2:3, :] * cvv)
    yc = cbr_ref[...].astype(F32) * conv3 * cgr_ref[...].astype(F32)
    u_ref[:, SSM_INNER + POOL_WIDTH:SSM_INNER + POOL_WIDTH + SC_WIDTH] = yc.astype(BF16)
    cv_ext[0:CONV_HALO, :] = cv_ext[L:L + CONV_HALO, :]

    def conv_silu(cur_ref, c0, width):
        xbc_ext[L:2 * L, c0:c0 + width] = cur_ref[...]
        sh = jnp.dot(shift_ref[...], xbc_ext[:, c0:c0 + width], preferred_element_type=F32)
        conv = cbias_ref[:, c0:c0 + width] + cw_ref[SSM_CONV - 1:SSM_CONV, c0:c0 + width] * cur_ref[...].astype(F32)
        for k in range(1, SSM_CONV):
            tap = SSM_CONV - 1 - k
            conv = conv + cw_ref[tap:tap + 1, c0:c0 + width] * sh[(k - 1) * L:k * L, :]
        half = 0.5 * conv
        return half * jnp.tanh(half) + half

    xs = conv_silu(x_ref, 0, SSM_INNER)
    bm = conv_silu(b_ref, SSM_INNER, SSM_GROUPS * SSM_STATE)
    cm = conv_silu(c_ref, SSM_INNER + SSM_GROUPS * SSM_STATE, SSM_GROUPS * SSM_STATE)
    xbc_ext[L - BF16_ROWS:L, :] = xbc_ext[2 * L - BF16_ROWS:2 * L, :]

    dt = _softplus(dt_ref[...] + dtb_ref[...])
    a2 = -jnp.exp(alog_ref[...]) * LOG2E
    log_a2 = dt * a2
    tril = jnp.where(row >= col, 1.0, 0.0).astype(BF16)
    acum = _dot01_left(tril, log_a2)
    acum_t = acum.T
    dt_t = dt.T
    a_last_col = acum_t[:, L - 1:L]
    w_t = jnp.exp2(a_last_col - acum_t) * dt_t
    arow_all = acum_t - jnp.log2(dt_t)
    a_last_rows = _dot01_right(acum[L - BF16_ROWS:L, :], e_ref[...])
    cd = jnp.exp2(a_last_rows[BF16_ROWS - 1:BF16_ROWS, :])
    neg_mask = jnp.where(row >= col, 0.0, MASK_EXPONENT)
    left = col < SSM_HEAD_DIM

    zs = z_ref[...].astype(F32)
    for g in range(SSM_GROUPS):
        n0 = g * SSM_STATE
        c_g = cm[:, n0:n0 + SSM_STATE].astype(BF16)
        bt_g = bm[:, n0:n0 + SSM_STATE].T
        cb_g = jnp.dot(c_g, bt_g.astype(BF16), preferred_element_type=F32)
        g0 = g * 512
        s_g = state_ref[:, g0:g0 + 512]
        yoff_g = jnp.dot(c_g, s_g.astype(BF16), preferred_element_type=F32)
        ygs = []
        for q in range(4):
            j = 4 * g + q
            lane0 = 128 * j
            lhs_parts = []
            acols = []
            for hh in range(2):
                h = 2 * j + hh
                acol = acum[:, h:h + 1]
                acols.append(acol)
                decay_dt = jnp.exp2((acol + neg_mask) - arow_all[h:h + 1, :])
                m = (cb_g * decay_dt).astype(BF16)
                btw = (bt_g * w_t[h:h + 1, :]).astype(BF16)
                lhs_parts.append(jnp.concatenate([m, btw], axis=0))
            lhs = jnp.concatenate(lhs_parts, axis=1)
            xp = xs[:, lane0:lane0 + 128]
            rhs = jnp.concatenate(
                [jnp.where(left, xp, 0.0).astype(BF16),
                 jnp.where(left, 0.0, xp).astype(BF16)], axis=0)
            r = jnp.dot(lhs, rhs, preferred_element_type=F32)
            ydiag = r[0:L, :]
            st_new = r[L:2 * L, :]
            fs = jnp.exp2(jnp.where(left, acols[0], acols[1]))
            y = (ydiag + yoff_g[:, 128 * q:128 * q + 128] * fs
                 + xp * dexp_ref[:, lane0:lane0 + 128])
            ygs.append(y * zs[:, lane0:lane0 + 128])
            state_ref[:, lane0:lane0 + 128] = (
                s_g[:, 128 * q:128 * q + 128] * cd[:, lane0:lane0 + 128] + st_new)
        yg = jnp.concatenate(ygs, axis=1)
        ms = jnp.mean(yg * yg, axis=1, keepdims=True)
        yb = yg * lax.rsqrt(ms + NORM_EPS) * nw_ref[:, g0:g0 + 512]
        u_ref[:, g0:g0 + 512] = yb.astype(BF16)


def _mixers(proj, dtp, layer, params, batch, seq):
    t = proj.shape[0]
    L = CHUNK
    nc = seq // L
    (poolw, pscale, cw, cbias, dtb, alog, dexp, nw, scw, emat, band, shift) = params

    def rows(width, off):
        blk = off // width
        return pl.BlockSpec((L, width), lambda b, c: (b * nc + c, blk))

    def lyr(shape):
        nd = len(shape)
        return pl.BlockSpec((None,) + shape, lambda b, c: (layer,) + (0,) * nd)

    in_specs = [
        rows(1024, OFF_PU), rows(1024, OFF_PG), rows(2048, OFF_Z), rows(2048, OFF_X),
        rows(512, OFF_B), rows(512, OFF_C),
        pl.BlockSpec((L, DT_PAD), lambda b, c: (b * nc + c, 0)),
        rows(1024, OFF_CB), rows(1024, OFF_CC), rows(1024, OFF_CV), rows(1024, OFF_CG),
        lyr((SSM_GROUPS, POOL_GROUP_DIM, POOL_GROUP_DIM)),
        lyr((1, POOL_WIDTH)),
        lyr((SSM_CONV, SSM_CONV_DIM)),
        lyr((1, SSM_CONV_DIM)),
        lyr((1, DT_PAD)),
        lyr((1, DT_PAD)),
        lyr((1, SSM_INNER)),
        lyr((1, SSM_INNER)),
        lyr((SC_KERNEL, SC_WIDTH)),
        pl.BlockSpec((DT_PAD, SSM_INNER), lambda b, c: (0, 0)),
        pl.BlockSpec((len(POOL_WINDOWS), L, 2 * L), lambda b, c: (0, 0, 0)),
        pl.BlockSpec(((SSM_CONV - 1) * L, 2 * L), lambda b, c: (0, 0)),
    ]
    ucols = SSM_INNER + POOL_WIDTH + SC_WIDTH
    return pl.pallas_call(
        _mixer_body,
        grid=(batch, nc),
        in_specs=in_specs,
        out_specs=pl.BlockSpec((L, ucols), lambda b, c: (b * nc + c, 0)),
        out_shape=jax.ShapeDtypeStruct((t, ucols), BF16),
        scratch_shapes=[
            pltpu.VMEM((2 * L, POOL_WIDTH), BF16),
            pltpu.VMEM((2 * L, SSM_CONV_DIM), BF16),
            pltpu.VMEM((CONV_HALO + L, SC_WIDTH), F32),
            pltpu.VMEM((SSM_STATE, SSM_INNER), F32),
        ],
        compiler_params=_cparams(("arbitrary", "arbitrary")),
        name="mixers",
    )(proj, proj, proj, proj, proj, proj, dtp, proj, proj, proj, proj,
      poolw, pscale, cw, cbias, dtb, alog, dexp, nw, scw, emat, band, shift)


def _merge_body(ub_ref, ua_ref, uc_ref, wa_ref, wb_ref, wc_ref, g0_ref, g1_ref, g2_ref, wo_ref,
                o_ref, wo_bf_ref, wa_scr, wb_scr, wc_scr):
    @pl.when(pl.program_id(1) == 0)
    def _():
        _cast_rows(wa_scr, wa_ref)
        _cast_rows(wb_scr, wb_ref)
        _cast_rows(wc_scr, wc_ref)

    wo_bf_ref[...] = wo_ref[...].astype(BF16)

    ya = jnp.dot(ua_ref[...], wa_scr[...], preferred_element_type=F32)
    yb = jnp.dot(ub_ref[...], wb_scr[...], preferred_element_type=F32)
    yc = jnp.dot(uc_ref[...], wc_scr[...], preferred_element_type=F32)
    acc = (g0_ref[...].astype(F32) * ya + g1_ref[...].astype(F32) * yb
           + g2_ref[...].astype(F32) * yc)
    o_ref[...] = acc.astype(o_ref.dtype)


def _merge(u, proj, wa, wb, wc, wo, layer, tm, tn):
    t = u.shape[0]
    gate_blk = OFF_GATE // tn
    per_gate = D_MODEL // tn
    n_blocks, m_steps = D_MODEL // tn, t // tm
    slab = D_MODEL // (n_blocks * m_steps)
    assert slab % BF16_ROWS == 0

    def weight(k):
        return pl.BlockSpec((None, k, tn), lambda i, j: (layer, 0, i), pipeline_mode=pl.Buffered(1))

    return pl.pallas_call(
        _merge_body,
        grid=(n_blocks, m_steps),
        in_specs=[
            pl.BlockSpec((tm, SSM_INNER), lambda i, j: (j, 0)),
            pl.BlockSpec((tm, POOL_WIDTH), lambda i, j: (j, SSM_INNER // POOL_WIDTH)),
            pl.BlockSpec((tm, SC_WIDTH), lambda i, j: (j, (SSM_INNER + POOL_WIDTH) // SC_WIDTH)),
            weight(POOL_WIDTH), weight(SSM_INNER), weight(SC_WIDTH),
            pl.BlockSpec((tm, tn), lambda i, j: (j, gate_blk + i)),
            pl.BlockSpec((tm, tn), lambda i, j: (j, gate_blk + per_gate + i)),
            pl.BlockSpec((tm, tn), lambda i, j: (j, gate_blk + 2 * per_gate + i)),
            pl.BlockSpec((None, slab, D_MODEL), lambda i, j: (layer, i * m_steps + j, 0)),
        ],
        out_specs=[
            pl.BlockSpec((tm, tn), lambda i, j: (j, i)),
            pl.BlockSpec((slab, D_MODEL), lambda i, j: (i * m_steps + j, 0)),
        ],
        out_shape=[jax.ShapeDtypeStruct((t, D_MODEL), BF16),
                   jax.ShapeDtypeStruct((D_MODEL, D_MODEL), BF16)],
        scratch_shapes=[pltpu.VMEM((POOL_WIDTH, tn), BF16), pltpu.VMEM((SSM_INNER, tn), BF16),
                        pltpu.VMEM((SC_WIDTH, tn), BF16)],
        compiler_params=_cparams(("arbitrary", "arbitrary")),
        name="merge",
    )(u, u, u, wa, wb, wc, proj, proj, proj, wo)


def _out_body(m_ref, w_ref, x_ref, nw_ref, *out_refs, emit_x):
    xn = x_ref[...] + jnp.dot(m_ref[...], w_ref[...], preferred_element_type=F32)
    ms = jnp.mean(xn * xn, axis=-1, keepdims=True)
    hn = xn * lax.rsqrt(ms + NORM_EPS) * nw_ref[...]
    if emit_x:
        out_refs[0][...] = xn
        out_refs[1][...] = hn.astype(out_refs[1].dtype)
    else:
        out_refs[0][...] = hn.astype(out_refs[0].dtype)


def _out_proj(merged, wo_bf, x, nw, nw_layer, last, tm):
    t, d = x.shape
    row_spec = pl.BlockSpec((tm, d), lambda j: (j, 0))
    if last:
        out_shape = [jax.ShapeDtypeStruct((t, d), F32)]
        out_specs = [row_spec]
    else:
        out_shape = [jax.ShapeDtypeStruct((t, d), F32), jax.ShapeDtypeStruct((t, d), BF16)]
        out_specs = [row_spec, row_spec]
    return pl.pallas_call(
        functools.partial(_out_body, emit_x=not last),
        grid=(t // tm,),
        in_specs=[
            row_spec,
            pl.BlockSpec((d, d), lambda j: (0, 0)),
            row_spec,
            pl.BlockSpec((None, 1, d), lambda j: (nw_layer, 0, 0)),
        ],
        out_specs=out_specs,
        out_shape=out_shape,
        compiler_params=_cparams(("arbitrary",)),
        name="out_proj",
    )(merged, wo_bf, x, nw)


def kernel(x, norm_w, w_in, b_gate, pool_w, pool_scale, ssm_conv_w, ssm_conv_b, ssm_dt_bias,
           ssm_a_log, ssm_d, ssm_norm_w, sc_conv_w, w_br_pool, w_br_ssm, w_br_conv, w_out,
           final_norm_w):
    batch, seq, d = x.shape
    depth = w_in.shape[0]
    t = batch * seq
    assert d == D_MODEL and seq % CHUNK == 0
    tm = min(1024, t)
    tm_small = min(512, t)
    tn = 1024

    xf = x.reshape(t, d)

    w_in_t = jnp.swapaxes(w_in.astype(F32), 1, 2)
    bias = jnp.concatenate([jnp.zeros((depth, OFF_GATE), F32), b_gate.astype(F32)], axis=1)[:, None, :]
    wa = w_br_pool.astype(F32)
    wb = w_br_ssm.astype(F32)
    wc = w_br_conv.astype(F32)
    wo = w_out.astype(F32)

    pad_h = ((0, 0), (0, DT_PAD - SSM_HEADS))
    head_of_lane = jnp.arange(SSM_INNER, dtype=jnp.int32) // SSM_HEAD_DIM
    emat = (jnp.arange(DT_PAD, dtype=jnp.int32)[:, None] == head_of_lane[None, :]).astype(BF16)
    t_idx = jnp.arange(CHUNK, dtype=jnp.int32)[:, None] + CHUNK
    k_idx = jnp.arange(2 * CHUNK, dtype=jnp.int32)[None, :]
    band = jnp.stack([jnp.logical_and(k_idx > t_idx - win, k_idx <= t_idx) for win in POOL_WINDOWS]
                     ).astype(BF16)
    shift = jnp.concatenate([k_idx == t_idx - k for k in range(1, SSM_CONV)], axis=0).astype(BF16)
    mix_params = (
        pool_w.astype(BF16),
        pool_scale.astype(F32)[:, None, :],
        ssm_conv_w.astype(F32),
        ssm_conv_b.astype(F32)[:, None, :],
        jnp.pad(ssm_dt_bias.astype(F32), pad_h)[:, None, :],
        jnp.pad(ssm_a_log.astype(F32), pad_h)[:, None, :],
        jnp.repeat(ssm_d.astype(F32), SSM_HEAD_DIM, axis=1)[:, None, :],
        ssm_norm_w.astype(F32)[:, None, :],
        sc_conv_w.astype(F32),
        emat, band, shift,
    )
    norm_all = jnp.concatenate([norm_w.astype(F32), final_norm_w.astype(F32)[None, :]], axis=0)[:, None, :]

    h = _rmsnorm(xf, norm_all, 0, BF16, tm_small)
    for i in range(depth):
        proj = _in_proj(h, w_in_t, bias, i, tm, tn)
        dtp = _dt_proj(h, w_in_t, i, tm)
        u = _mixers(proj, dtp, i, mix_params, batch, seq)
        merged, wo_bf = _merge(u, proj, wa, wb, wc, wo, i, tm_small, tn)
        last = i == depth - 1
        outs = _out_proj(merged, wo_bf, xf, norm_all, i + 1, last, tm_small)
        if last:
            return outs[0].reshape(batch, seq, d)
        xf, h = outs
```

```python
import functools

import jax
import jax.numpy as jnp
from jax import lax
from jax.experimental import pallas as pl
from jax.experimental.pallas import tpu as pltpu

F32 = jnp.float32
BF16 = jnp.bfloat16

NORM_EPS = 1e-6
D_MODEL = 2048
N_BRANCH = 3
POOL_WINDOWS = (2, 4, 8, 16)
POOL_WIDTH = 1024
POOL_GROUP_DIM = 256
SSM_INNER = 2048
SSM_HEAD_DIM = 64
SSM_HEADS = 32
SSM_GROUPS = 4
SSM_STATE = 128
SSM_CONV = 4
SSM_CONV_DIM = SSM_INNER + 2 * SSM_GROUPS * SSM_STATE
CHUNK = 128
SC_WIDTH = 1024
SC_KERNEL = 3

OFF_PU = 0
OFF_PG = 1024
OFF_Z = 2048
OFF_X = 4096
OFF_B = 6144
OFF_C = 6656
OFF_CB = 7168
OFF_CC = 8192
OFF_CV = 9216
OFF_CG = 10240
OFF_GATE = 11264
PROJ_COLS = OFF_GATE + N_BRANCH * D_MODEL
LANES = 128
DT_SRC = 7168
DT_PAD = LANES
DT_SHIFT = SSM_HEADS

F32_ROWS = 8
BF16_ROWS = 16
CONV_HALO = F32_ROWS
LOG2E = 1.4426950408889634
MASK_EXPONENT = -1e30

VMEM_LIMIT = 56 * 1024 * 1024


def _cparams(sem, flags=None):
    return pltpu.CompilerParams(dimension_semantics=sem, vmem_limit_bytes=VMEM_LIMIT, flags=flags)


def _cast_rows(dst_ref, src_ref, rows_per_chunk=256):
    n = src_ref.shape[0]
    for r in range(0, n, rows_per_chunk):
        dst_ref[r:r + rows_per_chunk, :] = src_ref[r:r + rows_per_chunk, :].astype(dst_ref.dtype)


def _rmsnorm_body(x_ref, w_ref, o_ref):
    x = x_ref[...]
    ms = jnp.mean(x * x, axis=-1, keepdims=True)
    o_ref[...] = (x * lax.rsqrt(ms + NORM_EPS) * w_ref[...]).astype(o_ref.dtype)


def _rmsnorm(x, w, layer, out_dtype, tm):
    t, d = x.shape
    return pl.pallas_call(
        _rmsnorm_body,
        grid=(t // tm,),
        in_specs=[
            pl.BlockSpec((tm, d), lambda i: (i, 0)),
            pl.BlockSpec((None, 1, d), lambda i: (layer, 0, 0)),
        ],
        out_specs=pl.BlockSpec((tm, d), lambda i: (i, 0)),
        out_shape=jax.ShapeDtypeStruct((t, d), out_dtype),
        compiler_params=_cparams(("arbitrary",)),
        name="rmsnorm",
    )(x, w)


_NT = (((1,), (1,)), ((), ()))


def _in_proj_body(h_ref, w_ref, b_ref, o_ref, w_scr, *, tn):
    col0 = pl.program_id(0) * tn

    @pl.when(pl.program_id(1) == 0)
    def _():
        _cast_rows(w_scr, w_ref)

    is_silu = jnp.logical_or(
        jnp.logical_and(col0 >= OFF_PG, col0 < OFF_X),
        jnp.logical_and(col0 >= OFF_CG, col0 < OFF_GATE),
    )
    is_gate = col0 >= OFF_GATE

    def tile(kind):
        acc = lax.dot_general(h_ref[...], w_scr[...], _NT, preferred_element_type=F32)
        if kind == "gate":
            out = 0.5 * jnp.tanh(0.5 * (acc + b_ref[...])) + 0.5
        elif kind == "silu":
            half = 0.5 * acc
            out = half * jnp.tanh(half) + half
        else:
            out = acc
        o_ref[...] = out.astype(o_ref.dtype)

    pl.when(is_gate)(functools.partial(tile, "gate"))
    pl.when(is_silu)(functools.partial(tile, "silu"))
    pl.when(jnp.logical_not(jnp.logical_or(is_gate, is_silu)))(functools.partial(tile, "plain"))


def _in_proj(h, w_in_t, bias, layer, tm, tn):
    t, k = h.shape
    assert OFF_CB % tn == 0

    depth, n_src, _ = w_in_t.shape
    w_rows_flat = w_in_t.reshape(depth * n_src, k)

    def w_rows(i, j):
        start = i * tn
        row = layer * n_src + start + jnp.where(start >= OFF_CB, DT_SHIFT, 0)
        return (pl.multiple_of(row, DT_SHIFT), 0)

    return pl.pallas_call(
        functools.partial(_in_proj_body, tn=tn),
        grid=(PROJ_COLS // tn, t // tm),
        in_specs=[
            pl.BlockSpec((tm, k), lambda i, j: (j, 0)),
            pl.BlockSpec((pl.Element(tn), pl.Element(k)), w_rows),
            pl.BlockSpec((None, 1, tn), lambda i, j: (layer, 0, i)),
        ],
        out_specs=pl.BlockSpec((tm, tn), lambda i, j: (j, i)),
        out_shape=jax.ShapeDtypeStruct((t, PROJ_COLS), BF16),
        scratch_shapes=[pltpu.VMEM((tn, k), BF16)],
        compiler_params=_cparams(("arbitrary", "arbitrary")),
        name="in_proj",
    )(h, w_rows_flat, bias)


def _dt_proj_body(h_ref, w_ref, o_ref):
    o_ref[...] = lax.dot_general(h_ref[...], w_ref[...].astype(BF16), _NT,
                                 preferred_element_type=F32)


def _dt_proj(h, w_in_t, layer, tm):
    t, k = h.shape
    return pl.pallas_call(
        _dt_proj_body,
        grid=(t // tm,),
        in_specs=[
            pl.BlockSpec((tm, k), lambda j: (j, 0)),
            pl.BlockSpec((None, DT_PAD, k), lambda j: (layer, DT_SRC // DT_PAD, 0)),
        ],
        out_specs=pl.BlockSpec((tm, DT_PAD), lambda j: (j, 0)),
        out_shape=jax.ShapeDtypeStruct((t, DT_PAD), F32),
        compiler_params=_cparams(("arbitrary",)),
        name="dt_proj",
    )(h, w_in_t)


def _split3(v):
    hi = v.astype(BF16)
    r1 = v - hi.astype(F32)
    mid = r1.astype(BF16)
    lo = (r1 - mid.astype(F32)).astype(BF16)
    return hi, mid, lo


def _dot01_left(m01, v):
    hi, mid, lo = _split3(v)
    return (jnp.dot(m01, hi, preferred_element_type=F32)
            + jnp.dot(m01, mid, preferred_element_type=F32)
            + jnp.dot(m01, lo, preferred_element_type=F32))


def _dot01_right(v, m01):
    hi, mid, lo = _split3(v)
    return (jnp.dot(hi, m01, preferred_element_type=F32)
            + jnp.dot(mid, m01, preferred_element_type=F32)
            + jnp.dot(lo, m01, preferred_element_type=F32))


def _softplus(x):
    return jnp.maximum(x, 0.0) + jnp.log1p(jnp.exp(-jnp.abs(x)))


N_MIXER_ROW_INPUTS = 11
N_MIXER_CONSTS = 12


def _mixer_body(*refs, chunks_per_step):
    row_refs = refs[:N_MIXER_ROW_INPUTS]
    consts = refs[N_MIXER_ROW_INPUTS:N_MIXER_ROW_INPUTS + N_MIXER_CONSTS]
    u_ref = refs[N_MIXER_ROW_INPUTS + N_MIXER_CONSTS]
    scratch = refs[N_MIXER_ROW_INPUTS + N_MIXER_CONSTS + 1:]
    for sub in range(chunks_per_step):
        rows = pl.ds(sub * CHUNK, CHUNK)
        _mixer_chunk(pl.program_id(1) * chunks_per_step + sub, sub == 0,
                     *[r.at[rows] for r in row_refs], *consts, u_ref.at[rows], *scratch)


def _mixer_chunk(c, may_start_sequence, pu_ref, pg_ref, z_ref, x_ref, b_ref, c_ref, dt_ref,
                 cbr_ref, ccr_ref, cvr_ref, cgr_ref,
                 poolw_ref, pscale_ref, cw_ref, cbias_ref, dtb_ref, alog_ref,
                 dexp_ref, nw_ref, scw_ref, e_ref, band_ref, shift_ref,
                 u_ref,
                 pool_ext, xbc_ext, cv_ext, state_ref,
                 bt_scr, cbf_scr, acum_scr, arow_scr, wt_scr, cd_scr, mask_scr):
    L = CHUNK
    row = lax.broadcasted_iota(jnp.int32, (L, L), 0)
    col = lax.broadcasted_iota(jnp.int32, (L, L), 1)

    if may_start_sequence:
        @pl.when(c == 0)
        def _():
            pool_ext[0:L, :] = jnp.zeros((L, POOL_WIDTH), BF16)
            xbc_ext[0:L, :] = jnp.zeros((L, SSM_CONV_DIM), BF16)
            cv_ext[0:CONV_HALO, :] = jnp.zeros((CONV_HALO, SC_WIDTH), F32)
            state_ref[...] = jnp.zeros_like(state_ref)
            mask_scr[...] = jnp.where(row >= col, 0.0, MASK_EXPONENT)

    pool_ext[L:2 * L, :] = pu_ref[...]
    pos = (c * L + 1 + lax.broadcasted_iota(jnp.int32, (L, 1), 0)).astype(F32)
    for g, win in enumerate(POOL_WINDOWS):
        lo, hi = g * POOL_GROUP_DIM, (g + 1) * POOL_GROUP_DIM
        wsum = jnp.dot(band_ref[g], pool_ext[:, lo:hi], preferred_element_type=F32)
        mean = wsum / jnp.minimum(pos, float(win))
        d = (mean - pu_ref[:, lo:hi].astype(F32)).astype(BF16)
        ya = (jnp.dot(d, poolw_ref[g], preferred_element_type=F32)
              * pscale_ref[:, lo:hi] * pg_ref[:, lo:hi].astype(F32))
        u_ref[:, SSM_INNER + lo:SSM_INNER + hi] = ya.astype(BF16)
    pool_ext[L - BF16_ROWS:L, :] = pool_ext[2 * L - BF16_ROWS:2 * L, :]

    strip = 256
    for lo in range(0, SC_WIDTH, strip):
        hi = lo + strip
        cvv = ccr_ref[:, lo:hi].astype(F32) * cvr_ref[:, lo:hi].astype(F32)
        cv_ext[CONV_HALO:CONV_HALO + L, lo:hi] = cvv
        conv3 = (scw_ref[0:1, lo:hi] * cv_ext[CONV_HALO - 2:CONV_HALO - 2 + L, lo:hi]
                 + scw_ref[1:2, lo:hi] * cv_ext[CONV_HALO - 1:CONV_HALO - 1 + L, lo:hi]
                 + scw_ref[2:3, lo:hi] * cvv)
        yc = cbr_ref[:, lo:hi].astype(F32) * conv3 * cgr_ref[:, lo:hi].astype(F32)
        u0 = SSM_INNER + POOL_WIDTH
        u_ref[:, u0 + lo:u0 + hi] = yc.astype(BF16)
    cv_ext[0:CONV_HALO, :] = cv_ext[L:L + CONV_HALO, :]

    def conv_silu(cur_ref, c0, lo, hi):
        sh = jnp.dot(shift_ref[...], xbc_ext[:, c0 + lo:c0 + hi], preferred_element_type=F32)
        conv = (cbias_ref[:, c0 + lo:c0 + hi]
                + cw_ref[SSM_CONV - 1:SSM_CONV, c0 + lo:c0 + hi] * cur_ref[:, lo:hi].astype(F32))
        for k in range(1, SSM_CONV):
            tap = SSM_CONV - 1 - k
            conv = conv + cw_ref[tap:tap + 1, c0 + lo:c0 + hi] * sh[(k - 1) * L:k * L, :]
        half = 0.5 * conv
        return half * jnp.tanh(half) + half

    n_bc = SSM_GROUPS * SSM_STATE
    xbc_ext[L:2 * L, 0:SSM_INNER] = x_ref[...]
    xbc_ext[L:2 * L, SSM_INNER:SSM_INNER + n_bc] = b_ref[...]
    xbc_ext[L:2 * L, SSM_INNER + n_bc:SSM_CONV_DIM] = c_ref[...]

    dt = _softplus(dt_ref[...] + dtb_ref[...])
    a2 = -jnp.exp(alog_ref[...]) * LOG2E
    tril = jnp.where(row >= col, 1.0, 0.0).astype(BF16)
    acum = _dot01_left(tril, dt * a2)
    acum_scr[...] = acum
    acum_t = acum.T
    dt_t = dt.T
    a_last_col = acum_t[:, L - 1:L]
    wt_scr[...] = jnp.exp2(a_last_col - acum_t) * dt_t
    arow_scr[...] = acum_t - jnp.log2(dt_t)
    a_last_rows = _dot01_right(acum[L - BF16_ROWS:L, :], e_ref[...])
    cd_scr[...] = jnp.exp2(a_last_rows[BF16_ROWS - F32_ROWS:BF16_ROWS, :])
    left = col < SSM_HEAD_DIM

    for lo in range(0, n_bc, 256):
        bm = conv_silu(b_ref, SSM_INNER, lo, lo + 256)
        cm = conv_silu(c_ref, SSM_INNER + n_bc, lo, lo + 256)
        cbf_scr[:, lo:lo + 256] = cm.astype(BF16)
        for gg in range(2):
            bt_scr[lo // SSM_STATE + gg] = bm[:, gg * SSM_STATE:(gg + 1) * SSM_STATE].T

    for g in range(SSM_GROUPS):
        g0 = g * 512
        c_g = cbf_scr[:, g * SSM_STATE:(g + 1) * SSM_STATE]
        cb_g = jnp.dot(c_g, bt_scr[g].astype(BF16), preferred_element_type=F32)
        yoff_g = jnp.dot(c_g, state_ref[:, g0:g0 + 512].astype(BF16), preferred_element_type=F32)
        ygs = []
        for q in range(4):
            j = 4 * g + q
            lane0 = 128 * j
            if q % 2 == 0:
                xs2 = conv_silu(x_ref, 0, lane0, lane0 + 256)
            xp = xs2[:, (q % 2) * 128:(q % 2) * 128 + 128]
            lhs_parts = []
            acols = []
            for hh in range(2):
                h = 2 * j + hh
                acol = acum_scr[:, h:h + 1]
                acols.append(acol)
                decay_dt = jnp.exp2((acol + mask_scr[...]) - arow_scr[h:h + 1, :])
                m = (cb_g * decay_dt).astype(BF16)
                btw = (bt_scr[g] * wt_scr[h:h + 1, :]).astype(BF16)
                lhs_parts.append(jnp.concatenate([m, btw], axis=0))
            lhs = jnp.concatenate(lhs_parts, axis=1)
            rhs = jnp.concatenate(
                [jnp.where(left, xp, 0.0).astype(BF16),
                 jnp.where(left, 0.0, xp).astype(BF16)], axis=0)
            r = jnp.dot(lhs, rhs, preferred_element_type=F32)
            ydiag = r[0:L, :]
            st_new = r[L:2 * L, :]
            fs = jnp.exp2(jnp.where(left, acols[0], acols[1]))
            y = (ydiag + yoff_g[:, 128 * q:128 * q + 128] * fs
                 + xp * dexp_ref[:, lane0:lane0 + 128])
            ygs.append(y * z_ref[:, lane0:lane0 + 128].astype(F32))
            state_ref[:, lane0:lane0 + 128] = (
                state_ref[:, lane0:lane0 + 128]
                * cd_scr[F32_ROWS - 1:F32_ROWS, lane0:lane0 + 128] + st_new)
        yg = jnp.concatenate(ygs, axis=1)
        ms = jnp.mean(yg * yg, axis=1, keepdims=True)
        yb = yg * lax.rsqrt(ms + NORM_EPS) * nw_ref[:, g0:g0 + 512]
        u_ref[:, g0:g0 + 512] = yb.astype(BF16)
    xbc_ext[L - BF16_ROWS:L, :] = xbc_ext[2 * L - BF16_ROWS:2 * L, :]


def _mixers(proj, dtp, layer, params, batch, seq, chunks_per_step):
    t = proj.shape[0]
    L = CHUNK
    rows_per_step = L * chunks_per_step
    nc = seq // rows_per_step
    assert len(params) == N_MIXER_CONSTS
    (poolw, pscale, cw, cbias, dtb, alog, dexp, nw, scw, emat, band, shift) = params

    def rows(width, off):
        blk = off // width
        return pl.BlockSpec((rows_per_step, width), lambda b, c: (b * nc + c, blk))

    def lyr(shape):
        nd = len(shape)
        return pl.BlockSpec((None,) + shape, lambda b, c: (layer,) + (0,) * nd)

    in_specs = [
        rows(1024, OFF_PU), rows(1024, OFF_PG), rows(2048, OFF_Z), rows(2048, OFF_X),
        rows(512, OFF_B), rows(512, OFF_C),
        pl.BlockSpec((rows_per_step, DT_PAD), lambda b, c: (b * nc + c, 0)),
        rows(1024, OFF_CB), rows(1024, OFF_CC), rows(1024, OFF_CV), rows(1024, OFF_CG),
        lyr((SSM_GROUPS, POOL_GROUP_DIM, POOL_GROUP_DIM)),
        lyr((1, POOL_WIDTH)),
        lyr((SSM_CONV, SSM_CONV_DIM)),
        lyr((1, SSM_CONV_DIM)),
        lyr((1, DT_PAD)),
        lyr((1, DT_PAD)),
        lyr((1, SSM_INNER)),
        lyr((1, SSM_INNER)),
        lyr((SC_KERNEL, SC_WIDTH)),
        pl.BlockSpec((DT_PAD, SSM_INNER), lambda b, c: (0, 0)),
        pl.BlockSpec((len(POOL_WINDOWS), L, 2 * L), lambda b, c: (0, 0, 0)),
        pl.BlockSpec(((SSM_CONV - 1) * L, 2 * L), lambda b, c: (0, 0)),
    ]
    ucols = SSM_INNER + POOL_WIDTH + SC_WIDTH
    assert len(in_specs) == N_MIXER_ROW_INPUTS + N_MIXER_CONSTS
    return pl.pallas_call(
        functools.partial(_mixer_body, chunks_per_step=chunks_per_step),
        grid=(batch, nc),
        in_specs=in_specs,
        out_specs=pl.BlockSpec((rows_per_step, ucols), lambda b, c: (b * nc + c, 0)),
        out_shape=jax.ShapeDtypeStruct((t, ucols), BF16),
        scratch_shapes=[
            pltpu.VMEM((2 * L, POOL_WIDTH), BF16),
            pltpu.VMEM((2 * L, SSM_CONV_DIM), BF16),
            pltpu.VMEM((CONV_HALO + L, SC_WIDTH), F32),
            pltpu.VMEM((SSM_STATE, SSM_INNER), F32),
            pltpu.VMEM((SSM_GROUPS, SSM_STATE, L), F32),
            pltpu.VMEM((L, SSM_GROUPS * SSM_STATE), BF16),
            pltpu.VMEM((L, DT_PAD), F32),
            pltpu.VMEM((DT_PAD, L), F32),
            pltpu.VMEM((DT_PAD, L), F32),
            pltpu.VMEM((F32_ROWS, SSM_INNER), F32),
            pltpu.VMEM((L, L), F32),
        ],
        compiler_params=_cparams(("arbitrary", "arbitrary")),
        name="mixers",
    )(proj, proj, proj, proj, proj, proj, dtp, proj, proj, proj, proj,
      poolw, pscale, cw, cbias, dtb, alog, dexp, nw, scw, emat, band, shift)


def _merge_body(ub_ref, ua_ref, uc_ref, wa_ref, wb_ref, wc_ref, g0_ref, g1_ref, g2_ref, wo_ref,
                o_ref, wo_bf_ref, wa_scr, wb_scr, wc_scr):
    @pl.when(pl.program_id(1) == 0)
    def _():
        _cast_rows(wa_scr, wa_ref)
        _cast_rows(wb_scr, wb_ref)
        _cast_rows(wc_scr, wc_ref)

    wo_bf_ref[...] = wo_ref[...].astype(BF16)

    ya = jnp.dot(ua_ref[...], wa_scr[...], preferred_element_type=F32)
    yb = jnp.dot(ub_ref[...], wb_scr[...], preferred_element_type=F32)
    yc = jnp.dot(uc_ref[...], wc_scr[...], preferred_element_type=F32)
    acc = (g0_ref[...].astype(F32) * ya + g1_ref[...].astype(F32) * yb
           + g2_ref[...].astype(F32) * yc)
    o_ref[...] = acc.astype(o_ref.dtype)


def _merge(u, proj, wa, wb, wc, wo, layer, tm, tn):
    t = u.shape[0]
    gate_blk = OFF_GATE // tn
    per_gate = D_MODEL // tn
    n_blocks, m_steps = D_MODEL // tn, t // tm
    slab = D_MODEL // (n_blocks * m_steps)
    assert slab % BF16_ROWS == 0

    def weight(k):
        return pl.BlockSpec((None, k, tn), lambda i, j: (layer, 0, i), pipeline_mode=pl.Buffered(1))

    return pl.pallas_call(
        _merge_body,
        grid=(n_blocks, m_steps),
        in_specs=[
            pl.BlockSpec((tm, SSM_INNER), lambda i, j: (j, 0)),
            pl.BlockSpec((tm, POOL_WIDTH), lambda i, j: (j, SSM_INNER // POOL_WIDTH)),
            pl.BlockSpec((tm, SC_WIDTH), lambda i, j: (j, (SSM_INNER + POOL_WIDTH) // SC_WIDTH)),
            weight(POOL_WIDTH), weight(SSM_INNER), weight(SC_WIDTH),
            pl.BlockSpec((tm, tn), lambda i, j: (j, gate_blk + i)),
            pl.BlockSpec((tm, tn), lambda i, j: (j, gate_blk + per_gate + i)),
            pl.BlockSpec((tm, tn), lambda i, j: (j, gate_blk + 2 * per_gate + i)),
            pl.BlockSpec((None, slab, D_MODEL), lambda i, j: (layer, i * m_steps + j, 0)),
        ],
        out_specs=[
            pl.BlockSpec((tm, tn), lambda i, j: (j, i)),
            pl.BlockSpec((slab, D_MODEL), lambda i, j: (i * m_steps + j, 0)),
        ],
        out_shape=[jax.ShapeDtypeStruct((t, D_MODEL), BF16),
                   jax.ShapeDtypeStruct((D_MODEL, D_MODEL), BF16)],
        scratch_shapes=[pltpu.VMEM((POOL_WIDTH, tn), BF16), pltpu.VMEM((SSM_INNER, tn), BF16),
                        pltpu.VMEM((SC_WIDTH, tn), BF16)],
        compiler_params=_cparams(("arbitrary", "arbitrary")),
        name="merge",
    )(u, u, u, wa, wb, wc, proj, proj, proj, wo)


def _out_body(m_ref, w_ref, x_ref, nw_ref, *out_refs, emit_x):
    xn = x_ref[...] + jnp.dot(m_ref[...], w_ref[...], preferred_element_type=F32)
    ms = jnp.mean(xn * xn, axis=-1, keepdims=True)
    hn = xn * lax.rsqrt(ms + NORM_EPS) * nw_ref[...]
    if emit_x:
        out_refs[0][...] = xn
        out_refs[1][...] = hn.astype(out_refs[1].dtype)
    else:
        out_refs[0][...] = hn.astype(out_refs[0].dtype)


def _out_proj(merged, wo_bf, x, nw, nw_layer, last, tm):
    t, d = x.shape
    row_spec = pl.BlockSpec((tm, d), lambda j: (j, 0))
    if last:
        out_shape = [jax.ShapeDtypeStruct((t, d), F32)]
        out_specs = [row_spec]
    else:
        out_shape = [jax.ShapeDtypeStruct((t, d), F32), jax.ShapeDtypeStruct((t, d), BF16)]
        out_specs = [row_spec, row_spec]
    return pl.pallas_call(
        functools.partial(_out_body, emit_x=not last),
        grid=(t // tm,),
        in_specs=[
            row_spec,
            pl.BlockSpec((d, d), lambda j: (0, 0)),
            row_spec,
            pl.BlockSpec((None, 1, d), lambda j: (nw_layer, 0, 0)),
        ],
        out_specs=out_specs,
        out_shape=out_shape,
        compiler_params=_cparams(("arbitrary",)),
        name="out_proj",
    )(merged, wo_bf, x, nw)


def kernel(x, norm_w, w_in, b_gate, pool_w, pool_scale, ssm_conv_w, ssm_conv_b, ssm_dt_bias,
           ssm_a_log, ssm_d, ssm_norm_w, sc_conv_w, w_br_pool, w_br_ssm, w_br_conv, w_out,
           final_norm_w):
    batch, seq, d = x.shape
    depth = w_in.shape[0]
    t = batch * seq
    assert d == D_MODEL and seq % CHUNK == 0
    tm = min(1024, t)
    tm_small = min(512, t)
    tn = 1024

    xf = x.reshape(t, d)

    w_in_t = jnp.swapaxes(w_in.astype(F32), 1, 2)
    bias = jnp.concatenate([jnp.zeros((depth, OFF_GATE), F32), b_gate.astype(F32)], axis=1)[:, None, :]
    wa = w_br_pool.astype(F32)
    wb = w_br_ssm.astype(F32)
    wc = w_br_conv.astype(F32)
    wo = w_out.astype(F32)

    pad_h = ((0, 0), (0, DT_PAD - SSM_HEADS))
    head_of_lane = jnp.arange(SSM_INNER, dtype=jnp.int32) // SSM_HEAD_DIM
    emat = (jnp.arange(DT_PAD, dtype=jnp.int32)[:, None] == head_of_lane[None, :]).astype(BF16)
    t_idx = jnp.arange(CHUNK, dtype=jnp.int32)[:, None] + CHUNK
    k_idx = jnp.arange(2 * CHUNK, dtype=jnp.int32)[None, :]
    band = jnp.stack([jnp.logical_and(k_idx > t_idx - win, k_idx <= t_idx) for win in POOL_WINDOWS]
                     ).astype(BF16)
    shift = jnp.concatenate([k_idx == t_idx - k for k in range(1, SSM_CONV)], axis=0).astype(BF16)
    mix_params = (
        pool_w.astype(BF16),
        pool_scale.astype(F32)[:, None, :],
        ssm_conv_w.astype(F32),
        ssm_conv_b.astype(F32)[:, None, :],
        jnp.pad(ssm_dt_bias.astype(F32), pad_h)[:, None, :],
        jnp.pad(ssm_a_log.astype(F32), pad_h)[:, None, :],
        jnp.repeat(ssm_d.astype(F32), SSM_HEAD_DIM, axis=1)[:, None, :],
        ssm_norm_w.astype(F32)[:, None, :],
        sc_conv_w.astype(F32),
        emat, band, shift,
    )
    norm_all = jnp.concatenate([norm_w.astype(F32), final_norm_w.astype(F32)[None, :]], axis=0)[:, None, :]

    h = _rmsnorm(xf, norm_all, 0, BF16, tm_small)
    for i in range(depth):
        proj = _in_proj(h, w_in_t, bias, i, min(2048, t), tn)
        dtp = _dt_proj(h, w_in_t, i, tm)
        u = _mixers(proj, dtp, i, mix_params, batch, seq, 2 if seq % (2 * CHUNK) == 0 else 1)
        merged, wo_bf = _merge(u, proj, wa, wb, wc, wo, i, tm_small, tn)
        last = i == depth - 1
        outs = _out_proj(merged, wo_bf, xf, norm_all, i + 1, last, tm_small)
        if last:
            return outs[0].reshape(batch, seq, d)
        xf, h = outs
```

```python
import functools

import jax
import jax.numpy as jnp
from jax import lax
from jax.experimental import pallas as pl
from jax.experimental.pallas import tpu as pltpu

F32 = jnp.float32
BF16 = jnp.bfloat16

NORM_EPS = 1e-6
D_MODEL = 2048
N_BRANCH = 3
POOL_WINDOWS = (2, 4, 8, 16)
POOL_WIDTH = 1024
POOL_GROUP_DIM = 256
SSM_INNER = 2048
SSM_HEAD_DIM = 64
SSM_HEADS = 32
SSM_GROUPS = 4
SSM_STATE = 128
SSM_CONV = 4
SSM_CONV_DIM = SSM_INNER + 2 * SSM_GROUPS * SSM_STATE
CHUNK = 128
SC_WIDTH = 1024
SC_KERNEL = 3

OFF_PU = 0
OFF_PG = 1024
OFF_Z = 2048
OFF_X = 4096
OFF_B = 6144
OFF_C = 6656
OFF_CB = 7168
OFF_CC = 8192
OFF_CV = 9216
OFF_CG = 10240
OFF_GATE = 11264
PROJ_COLS = OFF_GATE + N_BRANCH * D_MODEL
LANES = 128
DT_SRC = 7168
DT_PAD = LANES
DT_SHIFT = SSM_HEADS

F32_ROWS = 8
BF16_ROWS = 16
CONV_HALO = F32_ROWS
LOG2E = 1.4426950408889634
MASK_EXPONENT = -1e30

VMEM_LIMIT = 56 * 1024 * 1024


def _cparams(sem, flags=None):
    return pltpu.CompilerParams(dimension_semantics=sem, vmem_limit_bytes=VMEM_LIMIT, flags=flags)


def _cast_rows(dst_ref, src_ref, rows_per_chunk=256):
    n = src_ref.shape[0]
    for r in range(0, n, rows_per_chunk):
        dst_ref[r:r + rows_per_chunk, :] = src_ref[r:r + rows_per_chunk, :].astype(dst_ref.dtype)


_NT = (((1,), (1,)), ((), ()))


def _rmsnorm_body(x_ref, w_ref, o_ref):
    x = x_ref[...]
    ms = jnp.mean(x * x, axis=-1, keepdims=True)
    o_ref[...] = (x * lax.rsqrt(ms + NORM_EPS) * w_ref[...]).astype(o_ref.dtype)


def _rmsnorm(x, w, layer, out_dtype, tm):
    t, d = x.shape
    return pl.pallas_call(
        _rmsnorm_body,
        grid=(t // tm,),
        in_specs=[
            pl.BlockSpec((tm, d), lambda i: (i, 0)),
            pl.BlockSpec((None, 1, d), lambda i: (layer, 0, 0)),
        ],
        out_specs=pl.BlockSpec((tm, d), lambda i: (i, 0)),
        out_shape=jax.ShapeDtypeStruct((t, d), out_dtype),
        compiler_params=_cparams(("arbitrary",)),
        name="rmsnorm",
    )(x, w)


def _dt_proj_body(h_ref, w_ref, o_ref):
    o_ref[...] = lax.dot_general(h_ref[...], w_ref[...].astype(BF16), _NT,
                                 preferred_element_type=F32)


def _dt_proj(h, w_in_t, layer, tm):
    t, k = h.shape
    return pl.pallas_call(
        _dt_proj_body,
        grid=(t // tm,),
        in_specs=[
            pl.BlockSpec((tm, k), lambda j: (j, 0)),
            pl.BlockSpec((None, DT_PAD, k), lambda j: (layer, DT_SRC // DT_PAD, 0)),
        ],
        out_specs=pl.BlockSpec((tm, DT_PAD), lambda j: (j, 0)),
        out_shape=jax.ShapeDtypeStruct((t, DT_PAD), F32),
        compiler_params=_cparams(("arbitrary",)),
        name="dt_proj",
    )(h, w_in_t)


def _in_proj_body(h_ref, w_ref, b_ref, o_ref, w_scr, *, tn):
    col0 = pl.program_id(0) * tn

    @pl.when(pl.program_id(1) == 0)
    def _():
        _cast_rows(w_scr, w_ref)

    is_silu = jnp.logical_or(
        jnp.logical_and(col0 >= OFF_PG, col0 < OFF_X),
        jnp.logical_and(col0 >= OFF_CG, col0 < OFF_GATE),
    )
    is_gate = col0 >= OFF_GATE

    def tile(kind):
        n_chunks = 8
        rows = h_ref.shape[0] // n_chunks
        for q in range(n_chunks):
            r = pl.ds(q * rows, rows)
            acc = lax.dot_general(h_ref[r, :], w_scr[...], _NT, preferred_element_type=F32)
            if kind == "gate":
                out = 0.5 * jnp.tanh(0.5 * (acc + b_ref[...])) + 0.5
            elif kind == "silu":
                half = 0.5 * acc
                out = half * jnp.tanh(half) + half
            else:
                out = acc
            o_ref[r, :] = out.astype(o_ref.dtype)

    pl.when(is_gate)(functools.partial(tile, "gate"))
    pl.when(is_silu)(functools.partial(tile, "silu"))
    pl.when(jnp.logical_not(jnp.logical_or(is_gate, is_silu)))(functools.partial(tile, "plain"))


def _in_proj(h, w_in_t, bias, layer, tm, tn):
    t, k = h.shape
    assert OFF_CB % tn == 0

    depth, n_src, _ = w_in_t.shape
    w_rows_flat = w_in_t.reshape(depth * n_src, k)

    def w_rows(i, j):
        start = i * tn
        row = layer * n_src + start + jnp.where(start >= OFF_CB, DT_SHIFT, 0)
        return (pl.multiple_of(row, DT_SHIFT), 0)

    return pl.pallas_call(
        functools.partial(_in_proj_body, tn=tn),
        grid=(PROJ_COLS // tn, t // tm),
        in_specs=[
            pl.BlockSpec((tm, k), lambda i, j: (j, 0)),
            pl.BlockSpec((pl.Element(tn), pl.Element(k)), w_rows),
            pl.BlockSpec((None, 1, tn), lambda i, j: (layer, 0, i)),
        ],
        out_specs=pl.BlockSpec((tm, tn), lambda i, j: (j, i)),
        out_shape=jax.ShapeDtypeStruct((t, PROJ_COLS), BF16),
        scratch_shapes=[pltpu.VMEM((tn, k), BF16)],
        compiler_params=_cparams(("arbitrary", "arbitrary")),
        name="in_proj",
    )(h, w_rows_flat, bias)


def _split3(v):
    hi = v.astype(BF16)
    r1 = v - hi.astype(F32)
    mid = r1.astype(BF16)
    lo = (r1 - mid.astype(F32)).astype(BF16)
    return hi, mid, lo


def _dot01_right(v, m01):
    hi, mid, lo = _split3(v)
    return (jnp.dot(hi, m01, preferred_element_type=F32)
            + jnp.dot(mid, m01, preferred_element_type=F32)
            + jnp.dot(lo, m01, preferred_element_type=F32))


def _softplus(x):
    return jnp.maximum(x, 0.0) + jnp.log1p(jnp.exp(-jnp.abs(x)))


N_MIXER_ROW_INPUTS = 11
N_MIXER_CONSTS = 12
N_CAST_WEIGHTS = 4


def _mixer_body(*refs, chunks_per_step):
    n_in = N_MIXER_ROW_INPUTS + N_MIXER_CONSTS
    row_refs = refs[:N_MIXER_ROW_INPUTS]
    consts = refs[N_MIXER_ROW_INPUTS:n_in]
    w_f32_refs = refs[n_in:n_in + N_CAST_WEIGHTS]
    u_ref = refs[n_in + N_CAST_WEIGHTS]
    w_bf16_refs = refs[n_in + N_CAST_WEIGHTS + 1:n_in + 2 * N_CAST_WEIGHTS + 1]
    scratch = refs[n_in + 2 * N_CAST_WEIGHTS + 1:]
    for src, dst in zip(w_f32_refs, w_bf16_refs):
        dst[...] = src[...].astype(dst.dtype)
    for sub in range(chunks_per_step):
        rows = pl.ds(sub * CHUNK, CHUNK)
        _mixer_chunk(pl.program_id(1) * chunks_per_step + sub, sub == 0,
                     *[r.at[rows] for r in row_refs], *consts, u_ref.at[rows], *scratch)


def _mixer_chunk(c, may_start_sequence, pu_ref, pg_ref, z_ref, x_ref, b_ref, c_ref, dt_ref,
                 cbr_ref, ccr_ref, cvr_ref, cgr_ref,
                 poolw_ref, pscale_ref, cw_ref, cbias_ref, dtb_ref, alogc_ref,
                 dexp_ref, nw_ref, scw_ref, e_ref, band_ref, shift_ref,
                 u_ref,
                 pool_ext, xbc_ext, sc_ext, state_ref,
                 bt_scr, cbf_scr, acum_scr, arow_scr, wt_scr, cd_scr, mask_scr):
    L = CHUNK
    row = lax.broadcasted_iota(jnp.int32, (L, L), 0)
    col = lax.broadcasted_iota(jnp.int32, (L, L), 1)

    if may_start_sequence:
        @pl.when(c == 0)
        def _():
            pool_ext[0:L, :] = jnp.zeros((L, POOL_WIDTH), BF16)
            xbc_ext[0:L, :] = jnp.zeros((L, SSM_CONV_DIM), BF16)
            sc_ext[0:CONV_HALO, :] = jnp.zeros((CONV_HALO, SC_WIDTH), F32)
            state_ref[...] = jnp.zeros_like(state_ref)
            mask_scr[...] = jnp.where(row >= col, 0.0, MASK_EXPONENT)

    pool_ext[L:2 * L, :] = pu_ref[...]
    pos = (c * L + 1 + row).astype(F32)
    for g, win in enumerate(POOL_WINDOWS):
        lo, hi = g * POOL_GROUP_DIM, (g + 1) * POOL_GROUP_DIM
        wsum = jnp.dot(band_ref[g], pool_ext[:, lo:hi], preferred_element_type=F32)
        inv_n = 1.0 / jnp.minimum(pos, float(win))
        mean = wsum * jnp.concatenate([inv_n] * (POOL_GROUP_DIM // L), axis=1)
        d = (mean - pu_ref[:, lo:hi].astype(F32)).astype(BF16)
        ya = (jnp.dot(d, poolw_ref[g], preferred_element_type=F32)
              * pscale_ref[:, lo:hi] * pg_ref[:, lo:hi].astype(F32))
        u_ref[:, SSM_INNER + lo:SSM_INNER + hi] = ya.astype(BF16)
    pool_ext[L - BF16_ROWS:L, :] = pool_ext[2 * L - BF16_ROWS:2 * L, :]

    strip = 256
    for lo in range(0, SC_WIDTH, strip):
        hi = lo + strip
        cvv = ccr_ref[:, lo:hi].astype(F32) * cvr_ref[:, lo:hi].astype(F32)
        sc_ext[CONV_HALO:CONV_HALO + L, lo:hi] = cvv
        conv3 = (scw_ref[0:1, lo:hi] * sc_ext[CONV_HALO - 2:CONV_HALO - 2 + L, lo:hi]
                 + scw_ref[1:2, lo:hi] * sc_ext[CONV_HALO - 1:CONV_HALO - 1 + L, lo:hi]
                 + scw_ref[2:3, lo:hi] * cvv)
        yc = cbr_ref[:, lo:hi].astype(F32) * conv3 * cgr_ref[:, lo:hi].astype(F32)
        u0 = SSM_INNER + POOL_WIDTH
        u_ref[:, u0 + lo:u0 + hi] = yc.astype(BF16)
    sc_ext[0:CONV_HALO, :] = sc_ext[L:L + CONV_HALO, :]

    def conv_silu(cur_ref, c0, lo, hi):
        sh = jnp.dot(shift_ref[...], xbc_ext[:, c0 + lo:c0 + hi], preferred_element_type=F32)
        half = (0.5 * cbias_ref[:, c0 + lo:c0 + hi]
                + (0.5 * cw_ref[SSM_CONV - 1:SSM_CONV, c0 + lo:c0 + hi])
                * cur_ref[:, lo:hi].astype(F32))
        for k in range(1, SSM_CONV):
            tap = SSM_CONV - 1 - k
            half = half + (0.5 * cw_ref[tap:tap + 1, c0 + lo:c0 + hi]) * sh[(k - 1) * L:k * L, :]
        return half * jnp.tanh(half) + half

    n_bc = SSM_GROUPS * SSM_STATE
    xbc_ext[L:2 * L, 0:SSM_INNER] = x_ref[...]
    xbc_ext[L:2 * L, SSM_INNER:SSM_INNER + n_bc] = b_ref[...]
    xbc_ext[L:2 * L, SSM_INNER + n_bc:SSM_CONV_DIM] = c_ref[...]

    H = SSM_HEADS
    dt_t = _softplus((dt_ref[...] + dtb_ref[...]).T[0:H, :])
    a2 = -jnp.exp(alogc_ref[0:H, :]) * LOG2E
    triu = jnp.where(row <= col, 1.0, 0.0).astype(BF16)
    acum_t = _dot01_right(dt_t, triu) * a2
    a_last_col = acum_t[:, L - 1:L]
    wt_scr[...] = jnp.exp2(a_last_col - acum_t) * dt_t
    arow_scr[...] = acum_t - jnp.log2(dt_t)
    acum = jnp.concatenate([acum_t, jnp.zeros((L - H, L), F32)], axis=0).T
    acum_scr[...] = acum
    a_last_rows = _dot01_right(acum[L - BF16_ROWS:L, :], e_ref[...])
    cd_scr[...] = jnp.exp2(a_last_rows[BF16_ROWS - F32_ROWS:BF16_ROWS, :])
    left = col < SSM_HEAD_DIM

    for lo in range(0, n_bc, 256):
        bm = conv_silu(b_ref, SSM_INNER, lo, lo + 256)
        cm = conv_silu(c_ref, SSM_INNER + n_bc, lo, lo + 256)
        cbf_scr[:, lo:lo + 256] = cm.astype(BF16)
        for gg in range(2):
            bt_scr[lo // SSM_STATE + gg] = bm[:, gg * SSM_STATE:(gg + 1) * SSM_STATE].T

    for g in range(SSM_GROUPS):
        g0 = g * 512
        c_g = cbf_scr[:, g * SSM_STATE:(g + 1) * SSM_STATE]
        cb_g = jnp.dot(c_g, bt_scr[g].astype(BF16), preferred_element_type=F32)
        yoff_g = jnp.dot(c_g, state_ref[:, g0:g0 + 512].astype(BF16), preferred_element_type=F32)
        ygs = []
        for q in range(4):
            j = 4 * g + q
            lane0 = 128 * j
            if q % 2 == 0:
                xs2 = conv_silu(x_ref, 0, lane0, lane0 + 256)
            xp = xs2[:, (q % 2) * 128:(q % 2) * 128 + 128]
            lhs_parts = []
            acols = []
            for hh in range(2):
                h = 2 * j + hh
                acol = acum_scr[:, h:h + 1]
                acols.append(acol)
                decay_dt = jnp.exp2((acol + mask_scr[...]) - arow_scr[h:h + 1, :])
                m = (cb_g * decay_dt).astype(BF16)
                btw = (bt_scr[g] * wt_scr[h:h + 1, :]).astype(BF16)
                lhs_parts.append(jnp.concatenate([m, btw], axis=0))
            lhs = jnp.concatenate(lhs_parts, axis=1)
            rhs = jnp.concatenate(
                [jnp.where(left, xp, 0.0).astype(BF16),
                 jnp.where(left, 0.0, xp).astype(BF16)], axis=0)
            r = jnp.dot(lhs, rhs, preferred_element_type=F32)
            ydiag = r[0:L, :]
            st_new = r[L:2 * L, :]
            fs = jnp.exp2(jnp.where(left, acols[0], acols[1]))
            y = (ydiag + yoff_g[:, 128 * q:128 * q + 128] * fs
                 + xp * dexp_ref[:, lane0:lane0 + 128])
            ygs.append(y * z_ref[:, lane0:lane0 + 128].astype(F32))
            state_ref[:, lane0:lane0 + 128] = (
                state_ref[:, lane0:lane0 + 128]
                * cd_scr[F32_ROWS - 1:F32_ROWS, lane0:lane0 + 128] + st_new)
        yg = jnp.concatenate(ygs, axis=1)
        ms = jnp.mean(yg * yg, axis=1, keepdims=True)
        yb = yg * lax.rsqrt(ms + NORM_EPS) * nw_ref[:, g0:g0 + 512]
        u_ref[:, g0:g0 + 512] = yb.astype(BF16)
    xbc_ext[L - BF16_ROWS:L, :] = xbc_ext[2 * L - BF16_ROWS:2 * L, :]


def _mixers(proj, dtp, layer, params, cast_weights, batch, seq, chunks_per_step):
    t = proj.shape[0]
    L = CHUNK
    rows_per_step = L * chunks_per_step
    nc = seq // rows_per_step
    n_steps = batch * nc
    assert len(params) == N_MIXER_CONSTS and len(cast_weights) == N_CAST_WEIGHTS

    slab_in, slab_out, slab_shapes = [], [], []
    for w in cast_weights:
        _, w_rows, w_cols = w.shape
        slab = w_rows // n_steps
        assert slab * n_steps == w_rows and slab % BF16_ROWS == 0
        slab_in.append(pl.BlockSpec((None, slab, w_cols), lambda b, c: (layer, b * nc + c, 0)))
        slab_out.append(pl.BlockSpec((slab, w_cols), lambda b, c: (b * nc + c, 0)))
        slab_shapes.append(jax.ShapeDtypeStruct((w_rows, w_cols), BF16))
    (poolw, pscale, cw, cbias, dtb, alog, dexp, nw, scw, emat, band, shift) = params

    def rows(width, off):
        blk = off // width
        return pl.BlockSpec((rows_per_step, width), lambda b, c: (b * nc + c, blk))

    def lyr(shape):
        nd = len(shape)
        return pl.BlockSpec((None,) + shape, lambda b, c: (layer,) + (0,) * nd)

    in_specs = [
        rows(1024, OFF_PU), rows(1024, OFF_PG), rows(2048, OFF_Z), rows(2048, OFF_X),
        rows(512, OFF_B), rows(512, OFF_C),
        pl.BlockSpec((rows_per_step, DT_PAD), lambda b, c: (b * nc + c, 0)),
        rows(1024, OFF_CB), rows(1024, OFF_CC), rows(1024, OFF_CV), rows(1024, OFF_CG),
        lyr((SSM_GROUPS, POOL_GROUP_DIM, POOL_GROUP_DIM)),
        lyr((1, POOL_WIDTH)),
        lyr((SSM_CONV, SSM_CONV_DIM)),
        lyr((1, SSM_CONV_DIM)),
        lyr((1, DT_PAD)),
        lyr((SSM_HEADS, 1)),
        lyr((1, SSM_INNER)),
        lyr((1, SSM_INNER)),
        lyr((SC_KERNEL, SC_WIDTH)),
        pl.BlockSpec((DT_PAD, SSM_INNER), lambda b, c: (0, 0)),
        pl.BlockSpec((len(POOL_WINDOWS), L, 2 * L), lambda b, c: (0, 0, 0)),
        pl.BlockSpec(((SSM_CONV - 1) * L, 2 * L), lambda b, c: (0, 0)),
    ]
    ucols = SSM_INNER + POOL_WIDTH + SC_WIDTH
    assert len(in_specs) == N_MIXER_ROW_INPUTS + N_MIXER_CONSTS
    outs = pl.pallas_call(
        functools.partial(_mixer_body, chunks_per_step=chunks_per_step),
        grid=(batch, nc),
        in_specs=in_specs + slab_in,
        out_specs=[pl.BlockSpec((rows_per_step, ucols), lambda b, c: (b * nc + c, 0))] + slab_out,
        out_shape=[jax.ShapeDtypeStruct((t, ucols), BF16)] + slab_shapes,
        scratch_shapes=[
            pltpu.VMEM((2 * L, POOL_WIDTH), BF16),
            pltpu.VMEM((2 * L, SSM_CONV_DIM), BF16),
            pltpu.VMEM((CONV_HALO + L, SC_WIDTH), F32),
            pltpu.VMEM((SSM_STATE, SSM_INNER), F32),
            pltpu.VMEM((SSM_GROUPS, SSM_STATE, L), F32),
            pltpu.VMEM((L, SSM_GROUPS * SSM_STATE), BF16),
            pltpu.VMEM((L, DT_PAD), F32),
            pltpu.VMEM((SSM_HEADS, L), F32),
            pltpu.VMEM((SSM_HEADS, L), F32),
            pltpu.VMEM((F32_ROWS, SSM_INNER), F32),
            pltpu.VMEM((L, L), F32),
        ],
        compiler_params=_cparams(("arbitrary", "arbitrary")),
        name="mixers",
    )(proj, proj, proj, proj, proj, proj, dtp, proj, proj, proj, proj,
      poolw, pscale, cw, cbias, dtb, alog, dexp, nw, scw, emat, band, shift, *cast_weights)
    return outs[0], outs[1:]


def _merge_body(ub_ref, ua_ref, uc_ref, wa_ref, wb_ref, wc_ref, g0_ref, g1_ref, g2_ref, o_ref):
    ya = jnp.dot(ua_ref[...], wa_ref[...], preferred_element_type=F32)
    yb = jnp.dot(ub_ref[...], wb_ref[...], preferred_element_type=F32)
    yc = jnp.dot(uc_ref[...], wc_ref[...], preferred_element_type=F32)
    acc = (g0_ref[...].astype(F32) * ya + g1_ref[...].astype(F32) * yb
           + g2_ref[...].astype(F32) * yc)
    o_ref[...] = acc.astype(o_ref.dtype)


def _merge(u, proj, wa_bf, wb_bf, wc_bf, tm, tn):
    t = u.shape[0]
    gate_blk = OFF_GATE // tn
    per_gate = D_MODEL // tn
    return pl.pallas_call(
        _merge_body,
        grid=(D_MODEL // tn, t // tm),
        in_specs=[
            pl.BlockSpec((tm, SSM_INNER), lambda i, j: (j, 0)),
            pl.BlockSpec((tm, POOL_WIDTH), lambda i, j: (j, SSM_INNER // POOL_WIDTH)),
            pl.BlockSpec((tm, SC_WIDTH), lambda i, j: (j, (SSM_INNER + POOL_WIDTH) // SC_WIDTH)),
            pl.BlockSpec((POOL_WIDTH, tn), lambda i, j: (0, i)),
            pl.BlockSpec((SSM_INNER, tn), lambda i, j: (0, i)),
            pl.BlockSpec((SC_WIDTH, tn), lambda i, j: (0, i)),
            pl.BlockSpec((tm, tn), lambda i, j: (j, gate_blk + i)),
            pl.BlockSpec((tm, tn), lambda i, j: (j, gate_blk + per_gate + i)),
            pl.BlockSpec((tm, tn), lambda i, j: (j, gate_blk + 2 * per_gate + i)),
        ],
        out_specs=pl.BlockSpec((tm, tn), lambda i, j: (j, i)),
        out_shape=jax.ShapeDtypeStruct((t, D_MODEL), BF16),
        compiler_params=_cparams(("arbitrary", "arbitrary")),
        name="merge",
    )(u, u, u, wa_bf, wb_bf, wc_bf, proj, proj, proj)


def _out_body(m_ref, w_ref, x_ref, nw_ref, *out_refs, emit_x):
    xn = x_ref[...] + jnp.dot(m_ref[...], w_ref[...], preferred_element_type=F32)
    ms = jnp.mean(xn * xn, axis=-1, keepdims=True)
    hn = xn * lax.rsqrt(ms + NORM_EPS) * nw_ref[...]
    if emit_x:
        out_refs[0][...] = xn
        out_refs[1][...] = hn.astype(out_refs[1].dtype)
    else:
        out_refs[0][...] = hn.astype(out_refs[0].dtype)


def _out_proj(merged, wo_bf, x, nw, nw_layer, last, tm):
    t, d = x.shape
    row_spec = pl.BlockSpec((tm, d), lambda j: (j, 0))
    if last:
        out_shape = [jax.ShapeDtypeStruct((t, d), F32)]
        out_specs = [row_spec]
    else:
        out_shape = [jax.ShapeDtypeStruct((t, d), F32), jax.ShapeDtypeStruct((t, d), BF16)]
        out_specs = [row_spec, row_spec]
    return pl.pallas_call(
        functools.partial(_out_body, emit_x=not last),
        grid=(t // tm,),
        in_specs=[
            row_spec,
            pl.BlockSpec((d, d), lambda j: (0, 0)),
            row_spec,
            pl.BlockSpec((None, 1, d), lambda j: (nw_layer, 0, 0)),
        ],
        out_specs=out_specs,
        out_shape=out_shape,
        compiler_params=_cparams(("arbitrary",)),
        name="out_proj",
    )(merged, wo_bf, x, nw)


def kernel(x, norm_w, w_in, b_gate, pool_w, pool_scale, ssm_conv_w, ssm_conv_b, ssm_dt_bias,
           ssm_a_log, ssm_d, ssm_norm_w, sc_conv_w, w_br_pool, w_br_ssm, w_br_conv, w_out,
           final_norm_w):
    batch, seq, d = x.shape
    depth = w_in.shape[0]
    t = batch * seq
    assert d == D_MODEL and seq % CHUNK == 0
    tm_in = min(2048, t)
    tm_small = min(512, t)
    tn = 1024

    xf = x.reshape(t, d)

    w_in_t = jnp.swapaxes(w_in.astype(F32), 1, 2)
    bias = jnp.concatenate([jnp.zeros((depth, OFF_GATE), F32), b_gate.astype(F32)], axis=1)[:, None, :]
    wa = w_br_pool.astype(F32)
    wb = w_br_ssm.astype(F32)
    wc = w_br_conv.astype(F32)
    wo = w_out.astype(F32)

    pad_h = ((0, 0), (0, DT_PAD - SSM_HEADS))
    head_of_lane = jnp.arange(SSM_INNER, dtype=jnp.int32) // SSM_HEAD_DIM
    emat = (jnp.arange(DT_PAD, dtype=jnp.int32)[:, None] == head_of_lane[None, :]).astype(BF16)
    t_idx = jnp.arange(CHUNK, dtype=jnp.int32)[:, None] + CHUNK
    k_idx = jnp.arange(2 * CHUNK, dtype=jnp.int32)[None, :]
    band = jnp.stack([jnp.logical_and(k_idx > t_idx - win, k_idx <= t_idx) for win in POOL_WINDOWS]
                     ).astype(BF16)
    shift = jnp.concatenate([k_idx == t_idx - k for k in range(1, SSM_CONV)], axis=0).astype(BF16)
    mix_params = (
        pool_w.astype(BF16),
        pool_scale.astype(F32)[:, None, :],
        ssm_conv_w.astype(F32),
        ssm_conv_b.astype(F32)[:, None, :],
        jnp.pad(ssm_dt_bias.astype(F32), pad_h)[:, None, :],
        ssm_a_log.astype(F32)[:, :, None],
        jnp.repeat(ssm_d.astype(F32), SSM_HEAD_DIM, axis=1)[:, None, :],
        ssm_norm_w.astype(F32)[:, None, :],
        sc_conv_w.astype(F32),
        emat, band, shift,
    )
    norm_all = jnp.concatenate([norm_w.astype(F32), final_norm_w.astype(F32)[None, :]], axis=0)[:, None, :]

    h = _rmsnorm(xf, norm_all, 0, BF16, tm_small)
    for i in range(depth):
        proj = _in_proj(h, w_in_t, bias, i, tm_in, tn)
        dtp = _dt_proj(h, w_in_t, i, tm_in)
        u, (wa_bf, wb_bf, wc_bf, wo_bf) = _mixers(
            proj, dtp, i, mix_params, (wa, wb, wc, wo), batch, seq,
            2 if seq % (2 * CHUNK) == 0 else 1)
        merged = _merge(u, proj, wa_bf, wb_bf, wc_bf, tm_small, tn)
        last = i == depth - 1
        outs = _out_proj(merged, wo_bf, xf, norm_all, i + 1, last, tm_small)
        if last:
            return outs[0].reshape(batch, seq, d)
        xf, h = outs
```

```python
import functools

import jax
import jax.numpy as jnp
from jax import lax
from jax.experimental import pallas as pl
from jax.experimental.pallas import tpu as pltpu

F32 = jnp.float32
BF16 = jnp.bfloat16

NORM_EPS = 1e-6
D_MODEL = 2048
N_BRANCH = 3
POOL_WINDOWS = (2, 4, 8, 16)
POOL_WIDTH = 1024
POOL_GROUP_DIM = 256
SSM_INNER = 2048
SSM_HEAD_DIM = 64
SSM_HEADS = 32
SSM_GROUPS = 4
SSM_STATE = 128
SSM_CONV = 4
SSM_CONV_DIM = SSM_INNER + 2 * SSM_GROUPS * SSM_STATE
CHUNK = 128
SC_WIDTH = 1024
SC_KERNEL = 3

OFF_PU = 0
OFF_PG = 1024
OFF_Z = 2048
OFF_X = 4096
OFF_B = 6144
OFF_C = 6656
OFF_CB = 7168
OFF_CC = 8192
OFF_CV = 9216
OFF_CG = 10240
OFF_GATE = 11264
PROJ_COLS = OFF_GATE + N_BRANCH * D_MODEL
LANES = 128
DT_SRC = 7168
DT_PAD = LANES
DT_SHIFT = SSM_HEADS

F32_ROWS = 8
BF16_ROWS = 16
CONV_HALO = F32_ROWS
LOG2E = 1.4426950408889634
MASK_EXPONENT = -1e30

VMEM_LIMIT = 56 * 1024 * 1024


def _cparams(sem, flags=None):
    return pltpu.CompilerParams(dimension_semantics=sem, vmem_limit_bytes=VMEM_LIMIT, flags=flags)


def _cast_rows(dst_ref, src_ref, rows_per_chunk=256):
    n = src_ref.shape[0]
    for r in range(0, n, rows_per_chunk):
        dst_ref[r:r + rows_per_chunk, :] = src_ref[r:r + rows_per_chunk, :].astype(dst_ref.dtype)


_NT = (((1,), (1,)), ((), ()))


def _slab_cast_specs(weights, layer, n_steps):
    in_specs, out_specs, out_shapes = [], [], []
    for w in weights:
        _, w_rows, w_cols = w.shape
        slab = w_rows // n_steps
        assert slab * n_steps == w_rows and slab % BF16_ROWS == 0
        in_specs.append(pl.BlockSpec((None, slab, w_cols), lambda j: (layer, j, 0)))
        out_specs.append(pl.BlockSpec((slab, w_cols), lambda j: (j, 0)))
        out_shapes.append(jax.ShapeDtypeStruct((w_rows, w_cols), BF16))
    return in_specs, out_specs, out_shapes


def _cast_slabs(src_refs, dst_refs):
    for src, dst in zip(src_refs, dst_refs):
        dst[...] = src[...].astype(dst.dtype)


def _rmsnorm_body(x_ref, w_ref, *rest, n_cast):
    cast_src, o_ref, cast_dst = rest[:n_cast], rest[n_cast], rest[n_cast + 1:]
    _cast_slabs(cast_src, cast_dst)
    x = x_ref[...]
    ms = jnp.mean(x * x, axis=-1, keepdims=True)
    o_ref[...] = (x * lax.rsqrt(ms + NORM_EPS) * w_ref[...]).astype(o_ref.dtype)


def _rmsnorm(x, w, layer, tm, cast_weights):
    t, d = x.shape
    c_in, c_out, c_shapes = _slab_cast_specs(cast_weights, layer, t // tm)
    outs = pl.pallas_call(
        functools.partial(_rmsnorm_body, n_cast=len(cast_weights)),
        grid=(t // tm,),
        in_specs=[
            pl.BlockSpec((tm, d), lambda i: (i, 0)),
            pl.BlockSpec((None, 1, d), lambda i: (layer, 0, 0)),
        ] + c_in,
        out_specs=[pl.BlockSpec((tm, d), lambda i: (i, 0))] + c_out,
        out_shape=[jax.ShapeDtypeStruct((t, d), BF16)] + c_shapes,
        compiler_params=_cparams(("arbitrary",)),
        name="rmsnorm",
    )(x, w, *cast_weights)
    return outs[0], outs[1:]


def _dt_proj_body(h_ref, w_ref, o_ref):
    o_ref[...] = lax.dot_general(h_ref[...], w_ref[...].astype(BF16), _NT,
                                 preferred_element_type=F32)


def _dt_proj(h, w_in_t, layer, tm):
    t, k = h.shape
    return pl.pallas_call(
        _dt_proj_body,
        grid=(t // tm,),
        in_specs=[
            pl.BlockSpec((tm, k), lambda j: (j, 0)),
            pl.BlockSpec((None, DT_PAD, k), lambda j: (layer, DT_SRC // DT_PAD, 0)),
        ],
        out_specs=pl.BlockSpec((tm, DT_PAD), lambda j: (j, 0)),
        out_shape=jax.ShapeDtypeStruct((t, DT_PAD), F32),
        compiler_params=_cparams(("arbitrary",)),
        name="dt_proj",
    )(h, w_in_t)


def _in_proj_body(h_ref, w_ref, b_ref, o_ref, w_scr, *, tn):
    col0 = pl.program_id(0) * tn

    @pl.when(pl.program_id(1) == 0)
    def _():
        _cast_rows(w_scr, w_ref)

    is_silu = jnp.logical_or(
        jnp.logical_and(col0 >= OFF_PG, col0 < OFF_X),
        jnp.logical_and(col0 >= OFF_CG, col0 < OFF_GATE),
    )
    is_gate = col0 >= OFF_GATE

    def tile(kind):
        acc = lax.dot_general(h_ref[...], w_scr[...], _NT, preferred_element_type=F32)
        if kind == "gate":
            out = 0.5 * jnp.tanh(0.5 * (acc + b_ref[...])) + 0.5
        elif kind == "silu":
            half = 0.5 * acc
            out = half * jnp.tanh(half) + half
        else:
            out = acc
        o_ref[...] = out.astype(o_ref.dtype)

    pl.when(is_gate)(functools.partial(tile, "gate"))
    pl.when(is_silu)(functools.partial(tile, "silu"))
    pl.when(jnp.logical_not(jnp.logical_or(is_gate, is_silu)))(functools.partial(tile, "plain"))


def _in_proj(h, w_in_t, bias, layer, tm, tn):
    t, k = h.shape
    assert OFF_CB % tn == 0

    depth, n_src, _ = w_in_t.shape
    w_rows_flat = w_in_t.reshape(depth * n_src, k)

    def w_rows(i, j):
        start = i * tn
        row = layer * n_src + start + jnp.where(start >= OFF_CB, DT_SHIFT, 0)
        return (pl.multiple_of(row, DT_SHIFT), 0)

    return pl.pallas_call(
        functools.partial(_in_proj_body, tn=tn),
        grid=(PROJ_COLS // tn, t // tm),
        in_specs=[
            pl.BlockSpec((tm, k), lambda i, j: (j, 0)),
            pl.BlockSpec((pl.Element(tn), pl.Element(k)), w_rows),
            pl.BlockSpec((None, 1, tn), lambda i, j: (layer, 0, i)),
        ],
        out_specs=pl.BlockSpec((tm, tn), lambda i, j: (j, i)),
        out_shape=jax.ShapeDtypeStruct((t, PROJ_COLS), BF16),
        scratch_shapes=[pltpu.VMEM((tn, k), BF16)],
        compiler_params=_cparams(("arbitrary", "arbitrary")),
        name="in_proj",
    )(h, w_rows_flat, bias)


def _split3(v):
    hi = v.astype(BF16)
    r1 = v - hi.astype(F32)
    mid = r1.astype(BF16)
    lo = (r1 - mid.astype(F32)).astype(BF16)
    return hi, mid, lo


def _dot01_right(v, m01):
    hi, mid, lo = _split3(v)
    return (jnp.dot(hi, m01, preferred_element_type=F32)
            + jnp.dot(mid, m01, preferred_element_type=F32)
            + jnp.dot(lo, m01, preferred_element_type=F32))


def _softplus(x):
    return jnp.maximum(x, 0.0) + jnp.log1p(jnp.exp(-jnp.abs(x)))


N_MIXER_ROW_INPUTS = 11
N_MIXER_CONSTS = 12


def _mixer_chunk(c, may_start_sequence, pu_ref, pg_ref, z_ref, x_ref, b_ref, c_ref, dt_ref,
                 cbr_ref, ccr_ref, cvr_ref, cgr_ref,
                 poolw_ref, pscale_ref, cw_ref, cbias_ref, dtb_ref, alogc_ref,
                 dexp_ref, nw_ref, scw_ref, e_ref, band_ref, shift_ref,
                 u_ref,
                 pool_ext, xbc_ext, sc_ext, state_ref,
                 bt_scr, cbf_scr, acum_scr, arow_scr, wt_scr, cd_scr, mask_scr, after_group=None):
    L = CHUNK
    row = lax.broadcasted_iota(jnp.int32, (L, L), 0)
    col = lax.broadcasted_iota(jnp.int32, (L, L), 1)

    if may_start_sequence:
        @pl.when(c == 0)
        def _():
            pool_ext[0:L, :] = jnp.zeros((L, POOL_WIDTH), BF16)
            xbc_ext[0:L, :] = jnp.zeros((L, SSM_CONV_DIM), BF16)
            sc_ext[0:CONV_HALO, :] = jnp.zeros((CONV_HALO, SC_WIDTH), F32)
            state_ref[...] = jnp.zeros_like(state_ref)
            mask_scr[...] = jnp.where(row >= col, 0.0, MASK_EXPONENT)

    pool_ext[L:2 * L, :] = pu_ref[...]
    pos = (c * L + 1 + row).astype(F32)
    for g, win in enumerate(POOL_WINDOWS):
        lo, hi = g * POOL_GROUP_DIM, (g + 1) * POOL_GROUP_DIM
        wsum = jnp.dot(band_ref[g], pool_ext[:, lo:hi], preferred_element_type=F32)
        inv_n = 1.0 / jnp.minimum(pos, float(win))
        mean = wsum * jnp.concatenate([inv_n] * (POOL_GROUP_DIM // L), axis=1)
        d = (mean - pu_ref[:, lo:hi].astype(F32)).astype(BF16)
        ya = (jnp.dot(d, poolw_ref[g], preferred_element_type=F32)
              * pscale_ref[:, lo:hi] * pg_ref[:, lo:hi].astype(F32))
        u_ref[:, SSM_INNER + lo:SSM_INNER + hi] = ya.astype(BF16)
    pool_ext[L - BF16_ROWS:L, :] = pool_ext[2 * L - BF16_ROWS:2 * L, :]

    strip = 256
    for lo in range(0, SC_WIDTH, strip):
        hi = lo + strip
        cvv = ccr_ref[:, lo:hi].astype(F32) * cvr_ref[:, lo:hi].astype(F32)
        sc_ext[CONV_HALO:CONV_HALO + L, lo:hi] = cvv
        conv3 = (scw_ref[0:1, lo:hi] * sc_ext[CONV_HALO - 2:CONV_HALO - 2 + L, lo:hi]
                 + scw_ref[1:2, lo:hi] * sc_ext[CONV_HALO - 1:CONV_HALO - 1 + L, lo:hi]
                 + scw_ref[2:3, lo:hi] * cvv)
        yc = cbr_ref[:, lo:hi].astype(F32) * conv3 * cgr_ref[:, lo:hi].astype(F32)
        u0 = SSM_INNER + POOL_WIDTH
        u_ref[:, u0 + lo:u0 + hi] = yc.astype(BF16)
    sc_ext[0:CONV_HALO, :] = sc_ext[L:L + CONV_HALO, :]

    def conv_silu(cur_ref, c0, lo, hi):
        sh = jnp.dot(shift_ref[...], xbc_ext[:, c0 + lo:c0 + hi], preferred_element_type=F32)
        half = (0.5 * cbias_ref[:, c0 + lo:c0 + hi]
                + (0.5 * cw_ref[SSM_CONV - 1:SSM_CONV, c0 + lo:c0 + hi])
                * cur_ref[:, lo:hi].astype(F32))
        for k in range(1, SSM_CONV):
            tap = SSM_CONV - 1 - k
            half = half + (0.5 * cw_ref[tap:tap + 1, c0 + lo:c0 + hi]) * sh[(k - 1) * L:k * L, :]
        return half * jnp.tanh(half) + half

    n_bc = SSM_GROUPS * SSM_STATE
    xbc_ext[L:2 * L, 0:SSM_INNER] = x_ref[...]
    xbc_ext[L:2 * L, SSM_INNER:SSM_INNER + n_bc] = b_ref[...]
    xbc_ext[L:2 * L, SSM_INNER + n_bc:SSM_CONV_DIM] = c_ref[...]

    H = SSM_HEADS
    dt_t = _softplus((dt_ref[...] + dtb_ref[...]).T[0:H, :])
    a2 = -jnp.exp(alogc_ref[0:H, :]) * LOG2E
    triu = jnp.where(row <= col, 1.0, 0.0).astype(BF16)
    acum_t = _dot01_right(dt_t, triu) * a2
    a_last_col = acum_t[:, L - 1:L]
    wt_scr[...] = jnp.exp2(a_last_col - acum_t) * dt_t
    arow_scr[...] = acum_t - jnp.log2(dt_t)
    acum = jnp.concatenate([acum_t, jnp.zeros((L - H, L), F32)], axis=0).T
    acum_scr[...] = acum
    a_last_rows = _dot01_right(acum[L - BF16_ROWS:L, :], e_ref[...])
    cd_scr[...] = jnp.exp2(a_last_rows[BF16_ROWS - F32_ROWS:BF16_ROWS, :])
    left = col < SSM_HEAD_DIM

    for lo in range(0, n_bc, 256):
        bm = conv_silu(b_ref, SSM_INNER, lo, lo + 256)
        cm = conv_silu(c_ref, SSM_INNER + n_bc, lo, lo + 256)
        cbf_scr[:, lo:lo + 256] = cm.astype(BF16)
        for gg in range(2):
            bt_scr[lo // SSM_STATE + gg] = bm[:, gg * SSM_STATE:(gg + 1) * SSM_STATE].T

    for g in range(SSM_GROUPS):
        g0 = g * 512
        c_g = cbf_scr[:, g * SSM_STATE:(g + 1) * SSM_STATE]
        cb_g = jnp.dot(c_g, bt_scr[g].astype(BF16), preferred_element_type=F32)
        yoff_g = jnp.dot(c_g, state_ref[:, g0:g0 + 512].astype(BF16), preferred_element_type=F32)
        ygs = []
        for q in range(4):
            j = 4 * g + q
            lane0 = 128 * j
            if q % 2 == 0:
                xs2 = conv_silu(x_ref, 0, lane0, lane0 + 256)
            xp = xs2[:, (q % 2) * 128:(q % 2) * 128 + 128]
            lhs_parts = []
            acols = []
            for hh in range(2):
                h = 2 * j + hh
                acol = acum_scr[:, h:h + 1]
                acols.append(acol)
                decay_dt = jnp.exp2((acol + mask_scr[...]) - arow_scr[h:h + 1, :])
                m = (cb_g * decay_dt).astype(BF16)
                btw = (bt_scr[g] * wt_scr[h:h + 1, :]).astype(BF16)
                lhs_parts.append(jnp.concatenate([m, btw], axis=0))
            lhs = jnp.concatenate(lhs_parts, axis=1)
            rhs = jnp.concatenate(
                [jnp.where(left, xp, 0.0).astype(BF16),
                 jnp.where(left, 0.0, xp).astype(BF16)], axis=0)
            r = jnp.dot(lhs, rhs, preferred_element_type=F32)
            ydiag = r[0:L, :]
            st_new = r[L:2 * L, :]
            fs = jnp.exp2(jnp.where(left, acols[0], acols[1]))
            y = (ydiag + yoff_g[:, 128 * q:128 * q + 128] * fs
                 + xp * dexp_ref[:, lane0:lane0 + 128])
            ygs.append(y * z_ref[:, lane0:lane0 + 128].astype(F32))
            state_ref[:, lane0:lane0 + 128] = (
                state_ref[:, lane0:lane0 + 128]
                * cd_scr[F32_ROWS - 1:F32_ROWS, lane0:lane0 + 128] + st_new)
        yg = jnp.concatenate(ygs, axis=1)
        ms = jnp.mean(yg * yg, axis=1, keepdims=True)
        yb = yg * lax.rsqrt(ms + NORM_EPS) * nw_ref[:, g0:g0 + 512]
        u_ref[:, g0:g0 + 512] = yb.astype(BF16)
        if after_group is not None:
            after_group(g)
    xbc_ext[L - BF16_ROWS:L, :] = xbc_ext[2 * L - BF16_ROWS:2 * L, :]


MERGE_PIECE = 256
N_GATE_REFS = 2 * N_BRANCH


def _mixmerge_body(*refs, chunks_per_step, nc, n_steps):
    n_in = N_MIXER_ROW_INPUTS + N_MIXER_CONSTS
    row_refs = refs[:N_MIXER_ROW_INPUTS]
    consts = refs[N_MIXER_ROW_INPUTS:n_in]
    wo_ref, wa_ref, wb_ref, wc_ref = refs[n_in:n_in + 4]
    gate_refs = refs[n_in + 4:n_in + 4 + N_GATE_REFS]
    merged_ref, wo_bf_ref = refs[n_in + 4 + N_GATE_REFS:n_in + 6 + N_GATE_REFS]
    scratch = refs[n_in + 6 + N_GATE_REFS:]
    mix_scratch, (u_a, u_b) = scratch[:-2], scratch[-2:]
    s = pl.program_id(0)
    pieces_per_chunk = D_MODEL // MERGE_PIECE // chunks_per_step
    assert pieces_per_chunk == SSM_GROUPS
    gate_w = D_MODEL // 2

    wo_bf_ref[...] = wo_ref[...].astype(BF16)

    def merge_piece(u_prev, p):
        c0 = p * MERGE_PIECE
        cols = slice(c0, c0 + MERGE_PIECE)
        yb = jnp.dot(u_prev[:, 0:SSM_INNER], wb_ref[:, cols], preferred_element_type=F32)
        ya = jnp.dot(u_prev[:, SSM_INNER:SSM_INNER + POOL_WIDTH], wa_ref[:, cols],
                     preferred_element_type=F32)
        yc = jnp.dot(u_prev[:, SSM_INNER + POOL_WIDTH:], wc_ref[:, cols],
                     preferred_element_type=F32)
        half, g_lo = c0 // gate_w, c0 % gate_w
        g0, g1, g2 = [gate_refs[2 * k + half][:, g_lo:g_lo + MERGE_PIECE].astype(F32)
                      for k in range(N_BRANCH)]
        merged_ref[:, cols] = (g0 * ya + g1 * yb + g2 * yc).astype(merged_ref.dtype)

    def run(u_cur, u_prev):
        for sub in range(chunks_per_step):
            rows = pl.ds(sub * CHUNK, CHUNK)
            after = None
            if u_prev is not None:
                after = lambda g, sub=sub: merge_piece(u_prev, sub * pieces_per_chunk + g)
            if u_cur is not None:
                _mixer_chunk((s % nc) * chunks_per_step + sub, sub == 0,
                             *[r.at[rows] for r in row_refs], *consts, u_cur.at[rows],
                             *mix_scratch, after_group=after)
            else:
                for g in range(pieces_per_chunk):
                    after(g)

    steady = jnp.logical_and(s > 0, s < n_steps)
    pl.when(s == 0)(lambda: run(u_a, None))
    pl.when(jnp.logical_and(steady, s % 2 == 1))(lambda: run(u_b, u_a))
    pl.when(jnp.logical_and(steady, s % 2 == 0))(lambda: run(u_a, u_b))
    pl.when(s == n_steps)(lambda: run(None, u_a if n_steps % 2 == 1 else u_b))


def _mixmerge(proj, dtp, layer, params, wo, wa_bf, wb_bf, wc_bf, batch, seq, chunks_per_step):
    t = proj.shape[0]
    L = CHUNK
    rows_per_step = L * chunks_per_step
    nc = seq // rows_per_step
    n_steps = batch * nc
    assert len(params) == N_MIXER_CONSTS
    (poolw, pscale, cw, cbias, dtb, alog, dexp, nw, scw, emat, band, shift) = params
    slab = D_MODEL // n_steps
    assert slab * n_steps == D_MODEL and slab % BF16_ROWS == 0

    def cur(s):
        return jnp.minimum(s, n_steps - 1)

    def prev(s):
        return jnp.maximum(s - 1, 0)

    def rows(width, off):
        blk = off // width
        return pl.BlockSpec((rows_per_step, width), lambda s: (cur(s), blk))

    def lyr(shape):
        nd = len(shape)
        return pl.BlockSpec((None,) + shape, lambda s: (layer,) + (0,) * nd)

    def resident(shape):
        return pl.BlockSpec(shape, lambda s: (0, 0), pipeline_mode=pl.Buffered(1))

    gate_w = D_MODEL // 2
    gate_specs = [
        pl.BlockSpec((rows_per_step, gate_w),
                     lambda s, blk=(OFF_GATE + k * D_MODEL) // gate_w + half: (prev(s), blk))
        for k in range(N_BRANCH) for half in range(2)]
    in_specs = [
        rows(1024, OFF_PU), rows(1024, OFF_PG), rows(2048, OFF_Z), rows(2048, OFF_X),
        rows(512, OFF_B), rows(512, OFF_C),
        pl.BlockSpec((rows_per_step, DT_PAD), lambda s: (cur(s), 0)),
        rows(1024, OFF_CB), rows(1024, OFF_CC), rows(1024, OFF_CV), rows(1024, OFF_CG),
        lyr((SSM_GROUPS, POOL_GROUP_DIM, POOL_GROUP_DIM)),
        lyr((1, POOL_WIDTH)),
        lyr((SSM_CONV, SSM_CONV_DIM)),
        lyr((1, SSM_CONV_DIM)),
        lyr((1, DT_PAD)),
        lyr((SSM_HEADS, 1)),
        lyr((1, SSM_INNER)),
        lyr((1, SSM_INNER)),
        lyr((SC_KERNEL, SC_WIDTH)),
        pl.BlockSpec((DT_PAD, SSM_INNER), lambda s: (0, 0)),
        pl.BlockSpec((len(POOL_WINDOWS), L, 2 * L), lambda s: (0, 0, 0)),
        pl.BlockSpec(((SSM_CONV - 1) * L, 2 * L), lambda s: (0, 0)),
        pl.BlockSpec((None, slab, D_MODEL), lambda s: (layer, cur(s), 0)),
        resident((POOL_WIDTH, D_MODEL)), resident((SSM_INNER, D_MODEL)),
        resident((SC_WIDTH, D_MODEL)),
    ] + gate_specs
    ucols = SSM_INNER + POOL_WIDTH + SC_WIDTH
    return pl.pallas_call(
        functools.partial(_mixmerge_body, chunks_per_step=chunks_per_step, nc=nc, n_steps=n_steps),
        grid=(n_steps + 1,),
        in_specs=in_specs,
        out_specs=[pl.BlockSpec((rows_per_step, D_MODEL), lambda s: (prev(s), 0)),
                   pl.BlockSpec((slab, D_MODEL), lambda s: (cur(s), 0))],
        out_shape=[jax.ShapeDtypeStruct((t, D_MODEL), BF16),
                   jax.ShapeDtypeStruct((D_MODEL, D_MODEL), BF16)],
        scratch_shapes=[
            pltpu.VMEM((2 * L, POOL_WIDTH), BF16),
            pltpu.VMEM((2 * L, SSM_CONV_DIM), BF16),
            pltpu.VMEM((CONV_HALO + L, SC_WIDTH), F32),
            pltpu.VMEM((SSM_STATE, SSM_INNER), F32),
            pltpu.VMEM((SSM_GROUPS, SSM_STATE, L), F32),
            pltpu.VMEM((L, SSM_GROUPS * SSM_STATE), BF16),
            pltpu.VMEM((L, DT_PAD), F32),
            pltpu.VMEM((SSM_HEADS, L), F32),
            pltpu.VMEM((SSM_HEADS, L), F32),
            pltpu.VMEM((F32_ROWS, SSM_INNER), F32),
            pltpu.VMEM((L, L), F32),
            pltpu.VMEM((rows_per_step, ucols), BF16),
            pltpu.VMEM((rows_per_step, ucols), BF16),
        ],
        compiler_params=_cparams(("arbitrary",)),
        name="mixmerge",
    )(proj, proj, proj, proj, proj, proj, dtp, proj, proj, proj, proj,
      poolw, pscale, cw, cbias, dtb, alog, dexp, nw, scw, emat, band, shift,
      wo, wa_bf, wb_bf, wc_bf, *([proj] * N_GATE_REFS))


def _out_body(m_ref, w_ref, x_ref, nw_ref, *rest, last, n_cast):
    xn = x_ref[...] + jnp.dot(m_ref[...], w_ref[...], preferred_element_type=F32)
    ms = jnp.mean(xn * xn, axis=-1, keepdims=True)
    hn = xn * lax.rsqrt(ms + NORM_EPS) * nw_ref[...]
    if last:
        (o_ref,) = rest
        o_ref[...] = hn
    else:
        cast_src, (x_o_ref, h_o_ref), cast_dst = rest[:n_cast], rest[n_cast:n_cast + 2], rest[n_cast + 2:]
        _cast_slabs(cast_src, cast_dst)
        x_o_ref[...] = xn
        h_o_ref[...] = hn.astype(h_o_ref.dtype)


def _out_proj(merged, wo_bf, x, nw, next_layer, last, tm, cast_weights):
    t, d = x.shape
    row_spec = pl.BlockSpec((tm, d), lambda j: (j, 0))
    in_specs = [
        row_spec,
        pl.BlockSpec((d, d), lambda j: (0, 0)),
        row_spec,
        pl.BlockSpec((None, 1, d), lambda j: (next_layer, 0, 0)),
    ]
    operands = [merged, wo_bf, x, nw]
    if last:
        n_cast = 0
        out_shape = [jax.ShapeDtypeStruct((t, d), F32)]
        out_specs = [row_spec]
    else:
        n_cast = len(cast_weights)
        c_in, c_out, c_shapes = _slab_cast_specs(cast_weights, next_layer, t // tm)
        in_specs += c_in
        operands += list(cast_weights)
        out_shape = [jax.ShapeDtypeStruct((t, d), F32), jax.ShapeDtypeStruct((t, d), BF16)] + c_shapes
        out_specs = [row_spec, row_spec] + c_out
    outs = pl.pallas_call(
        functools.partial(_out_body, last=last, n_cast=n_cast),
        grid=(t // tm,),
        in_specs=in_specs,
        out_specs=out_specs,
        out_shape=out_shape,
        compiler_params=_cparams(("arbitrary",)),
        name="out_proj",
    )(*operands)
    if last:
        return outs
    return outs[0], outs[1], outs[2:]


def kernel(x, norm_w, w_in, b_gate, pool_w, pool_scale, ssm_conv_w, ssm_conv_b, ssm_dt_bias,
           ssm_a_log, ssm_d, ssm_norm_w, sc_conv_w, w_br_pool, w_br_ssm, w_br_conv, w_out,
           final_norm_w):
    batch, seq, d = x.shape
    depth = w_in.shape[0]
    t = batch * seq
    assert d == D_MODEL and seq % CHUNK == 0
    tm_in = min(2048, t)
    tm_small = min(512, t)
    tn = 1024

    xf = x.reshape(t, d)

    w_in_t = jnp.swapaxes(w_in.astype(F32), 1, 2)
    bias = jnp.concatenate([jnp.zeros((depth, OFF_GATE), F32), b_gate.astype(F32)], axis=1)[:, None, :]
    wa = w_br_pool.astype(F32)
    wb = w_br_ssm.astype(F32)
    wc = w_br_conv.astype(F32)
    wo = w_out.astype(F32)

    pad_h = ((0, 0), (0, DT_PAD - SSM_HEADS))
    head_of_lane = jnp.arange(SSM_INNER, dtype=jnp.int32) // SSM_HEAD_DIM
    emat = (jnp.arange(DT_PAD, dtype=jnp.int32)[:, None] == head_of_lane[None, :]).astype(BF16)
    t_idx = jnp.arange(CHUNK, dtype=jnp.int32)[:, None] + CHUNK
    k_idx = jnp.arange(2 * CHUNK, dtype=jnp.int32)[None, :]
    band = jnp.stack([jnp.logical_and(k_idx > t_idx - win, k_idx <= t_idx) for win in POOL_WINDOWS]
                     ).astype(BF16)
    shift = jnp.concatenate([k_idx == t_idx - k for k in range(1, SSM_CONV)], axis=0).astype(BF16)
    mix_params = (
        pool_w.astype(BF16),
        pool_scale.astype(F32)[:, None, :],
        ssm_conv_w.astype(F32),
        ssm_conv_b.astype(F32)[:, None, :],
        jnp.pad(ssm_dt_bias.astype(F32), pad_h)[:, None, :],
        ssm_a_log.astype(F32)[:, :, None],
        jnp.repeat(ssm_d.astype(F32), SSM_HEAD_DIM, axis=1)[:, None, :],
        ssm_norm_w.astype(F32)[:, None, :],
        sc_conv_w.astype(F32),
        emat, band, shift,
    )
    norm_all = jnp.concatenate([norm_w.astype(F32), final_norm_w.astype(F32)[None, :]], axis=0)[:, None, :]

    branch_w = (wa, wb, wc)
    h, (wa_bf, wb_bf, wc_bf) = _rmsnorm(xf, norm_all, 0, tm_small, branch_w)
    for i in range(depth):
        proj = _in_proj(h, w_in_t, bias, i, tm_in, tn)
        dtp = _dt_proj(h, w_in_t, i, tm_in)
        merged, wo_bf = _mixmerge(proj, dtp, i, mix_params, wo, wa_bf, wb_bf, wc_bf, batch, seq,
                                  2 if seq % (2 * CHUNK) == 0 else 1)
        last = i == depth - 1
        outs = _out_proj(merged, wo_bf, xf, norm_all, i + 1, last, tm_small, branch_w)
        if last:
            return outs[0].reshape(batch, seq, d)
        xf, h, (wa_bf, wb_bf, wc_bf) = outs
```

```python
import functools

import jax
import jax.numpy as jnp
from jax import lax
from jax.experimental import pallas as pl
from jax.experimental.pallas import tpu as pltpu

F32 = jnp.float32
BF16 = jnp.bfloat16

NORM_EPS = 1e-6
D_MODEL = 2048
N_BRANCH = 3
POOL_WINDOWS = (2, 4, 8, 16)
POOL_WIDTH = 1024
POOL_GROUP_DIM = 256
SSM_INNER = 2048
SSM_HEAD_DIM = 64
SSM_HEADS = 32
SSM_GROUPS = 4
SSM_STATE = 128
SSM_CONV = 4
SSM_CONV_DIM = SSM_INNER + 2 * SSM_GROUPS * SSM_STATE
CHUNK = 128
SC_WIDTH = 1024
SC_KERNEL = 3

OFF_PU = 0
OFF_PG = 1024
OFF_Z = 2048
OFF_X = 4096
OFF_B = 6144
OFF_C = 6656
OFF_CB = 7168
OFF_CC = 8192
OFF_CV = 9216
OFF_CG = 10240
OFF_GATE = 11264
PROJ_COLS = OFF_GATE + N_BRANCH * D_MODEL
LANES = 128
DT_SRC = 7168
DT_PAD = LANES
DT_SHIFT = SSM_HEADS

F32_ROWS = 8
BF16_ROWS = 16
CONV_HALO = F32_ROWS
LOG2E = 1.4426950408889634
MASK_EXPONENT = -1e30

VMEM_LIMIT = 56 * 1024 * 1024


def _cparams(sem):
    return pltpu.CompilerParams(dimension_semantics=sem, vmem_limit_bytes=VMEM_LIMIT)


def _cast_rows(dst_ref, src_ref, rows_per_chunk=256):
    n = src_ref.shape[0]
    for r in range(0, n, rows_per_chunk):
        dst_ref[r:r + rows_per_chunk, :] = src_ref[r:r + rows_per_chunk, :].astype(dst_ref.dtype)


_NT = (((1,), (1,)), ((), ()))


def _rmsnorm_body(x_ref, w_ref, o_ref):
    x = x_ref[...]
    ms = jnp.mean(x * x, axis=-1, keepdims=True)
    o_ref[...] = (x * lax.rsqrt(ms + NORM_EPS) * w_ref[...]).astype(o_ref.dtype)


def _rmsnorm(x, w, layer, out_dtype, tm):
    t, d = x.shape
    return pl.pallas_call(
        _rmsnorm_body,
        grid=(t // tm,),
        in_specs=[
            pl.BlockSpec((tm, d), lambda i: (i, 0)),
            pl.BlockSpec((None, 1, d), lambda i: (layer, 0, 0)),
        ],
        out_specs=pl.BlockSpec((tm, d), lambda i: (i, 0)),
        out_shape=jax.ShapeDtypeStruct((t, d), out_dtype),
        compiler_params=_cparams(("arbitrary",)),
        name="rmsnorm",
    )(x, w)


def _dt_proj_body(h_ref, w_ref, o_ref):
    o_ref[...] = lax.dot_general(h_ref[...], w_ref[...].astype(BF16), _NT,
                                 preferred_element_type=F32)


def _dt_proj(h, w_in_t, layer, tm):
    t, k = h.shape
    return pl.pallas_call(
        _dt_proj_body,
        grid=(t // tm,),
        in_specs=[
            pl.BlockSpec((tm, k), lambda j: (j, 0)),
            pl.BlockSpec((None, DT_PAD, k), lambda j: (layer, DT_SRC // DT_PAD, 0)),
        ],
        out_specs=pl.BlockSpec((tm, DT_PAD), lambda j: (j, 0)),
        out_shape=jax.ShapeDtypeStruct((t, DT_PAD), F32),
        compiler_params=_cparams(("arbitrary",)),
        name="dt_proj",
    )(h, w_in_t)


def _in_proj_body(h_ref, w_ref, b_ref, o_ref, w_scr, *, tn):
    col0 = pl.program_id(0) * tn

    @pl.when(pl.program_id(1) == 0)
    def _():
        _cast_rows(w_scr, w_ref)

    is_silu = jnp.logical_or(
        jnp.logical_and(col0 >= OFF_PG, col0 < OFF_X),
        jnp.logical_and(col0 >= OFF_CG, col0 < OFF_GATE),
    )
    is_gate = col0 >= OFF_GATE

    def tile(kind):
        acc = lax.dot_general(h_ref[...], w_scr[...], _NT, preferred_element_type=F32)
        if kind == "gate":
            out = 0.5 * jnp.tanh(0.5 * (acc + b_ref[...])) + 0.5
        elif kind == "silu":
            half = 0.5 * acc
            out = half * jnp.tanh(half) + half
        else:
            out = acc
        o_ref[...] = out.astype(o_ref.dtype)

    pl.when(is_gate)(functools.partial(tile, "gate"))
    pl.when(is_silu)(functools.partial(tile, "silu"))
    pl.when(jnp.logical_not(jnp.logical_or(is_gate, is_silu)))(functools.partial(tile, "plain"))


def _in_proj(h, w_in_t, bias, layer, tm, tn):
    t, k = h.shape
    assert OFF_CB % tn == 0

    depth, n_src, _ = w_in_t.shape
    w_rows_flat = w_in_t.reshape(depth * n_src, k)

    def w_rows(i, j):
        start = i * tn
        row = layer * n_src + start + jnp.where(start >= OFF_CB, DT_SHIFT, 0)
        return (pl.multiple_of(row, DT_SHIFT), 0)

    return pl.pallas_call(
        functools.partial(_in_proj_body, tn=tn),
        grid=(PROJ_COLS // tn, t // tm),
        in_specs=[
            pl.BlockSpec((tm, k), lambda i, j: (j, 0)),
            pl.BlockSpec((pl.Element(tn), pl.Element(k)), w_rows),
            pl.BlockSpec((None, 1, tn), lambda i, j: (layer, 0, i)),
        ],
        out_specs=pl.BlockSpec((tm, tn), lambda i, j: (j, i)),
        out_shape=jax.ShapeDtypeStruct((t, PROJ_COLS), BF16),
        scratch_shapes=[pltpu.VMEM((tn, k), BF16)],
        compiler_params=_cparams(("arbitrary", "arbitrary")),
        name="in_proj",
    )(h, w_rows_flat, bias)


def _split3(v):
    hi = v.astype(BF16)
    r1 = v - hi.astype(F32)
    mid = r1.astype(BF16)
    lo = (r1 - mid.astype(F32)).astype(BF16)
    return hi, mid, lo


def _dot01_right(v, m01):
    hi, mid, lo = _split3(v)
    return (jnp.dot(hi, m01, preferred_element_type=F32)
            + jnp.dot(mid, m01, preferred_element_type=F32)
            + jnp.dot(lo, m01, preferred_element_type=F32))


def _softplus(x):
    return jnp.maximum(x, 0.0) + jnp.log1p(jnp.exp(-jnp.abs(x)))


N_MIXER_ROW_INPUTS = 11
N_MIXER_CONSTS = 12
N_CAST_WEIGHTS = 4


def _mixer_body(*refs, chunks_per_step):
    n_in = N_MIXER_ROW_INPUTS + N_MIXER_CONSTS
    row_refs = refs[:N_MIXER_ROW_INPUTS]
    consts = refs[N_MIXER_ROW_INPUTS:n_in]
    w_f32_refs = refs[n_in:n_in + N_CAST_WEIGHTS]
    u_ref = refs[n_in + N_CAST_WEIGHTS]
    w_bf16_refs = refs[n_in + N_CAST_WEIGHTS + 1:n_in + 2 * N_CAST_WEIGHTS + 1]
    scratch = refs[n_in + 2 * N_CAST_WEIGHTS + 1:]
    for src, dst in zip(w_f32_refs, w_bf16_refs):
        dst[...] = src[...].astype(dst.dtype)
    for sub in range(chunks_per_step):
        rows = pl.ds(sub * CHUNK, CHUNK)
        _mixer_chunk(pl.program_id(1) * chunks_per_step + sub, sub == 0,
                     *[r.at[rows] for r in row_refs], *consts, u_ref.at[rows], *scratch)


def _mixer_chunk(c, may_start_sequence, pu_ref, pg_ref, z_ref, x_ref, b_ref, c_ref, dt_ref,
                 cbr_ref, ccr_ref, cvr_ref, cgr_ref,
                 poolw_ref, pscale_ref, cw_ref, cbias_ref, dtb_ref, alogc_ref,
                 dexp_ref, nw_ref, scw_ref, e_ref, band_ref, shift_ref,
                 u_ref,
                 pool_ext, xbc_ext, sc_ext, state_ref,
                 bt_scr, cbf_scr, acum_scr, arow_scr, wt_scr, cd_scr, mask_scr):
    L = CHUNK
    row = lax.broadcasted_iota(jnp.int32, (L, L), 0)
    col = lax.broadcasted_iota(jnp.int32, (L, L), 1)

    if may_start_sequence:
        @pl.when(c == 0)
        def _():
            pool_ext[0:L, :] = jnp.zeros((L, POOL_WIDTH), BF16)
            xbc_ext[0:L, :] = jnp.zeros((L, SSM_CONV_DIM), BF16)
            sc_ext[0:CONV_HALO, :] = jnp.zeros((CONV_HALO, SC_WIDTH), F32)
            state_ref[...] = jnp.zeros_like(state_ref)
            mask_scr[...] = jnp.where(row >= col, 0.0, MASK_EXPONENT)

    pool_ext[L:2 * L, :] = pu_ref[...]
    pos = (c * L + 1 + row).astype(F32)
    for g, win in enumerate(POOL_WINDOWS):
        lo, hi = g * POOL_GROUP_DIM, (g + 1) * POOL_GROUP_DIM
        wsum = jnp.dot(band_ref[g], pool_ext[:, lo:hi], preferred_element_type=F32)
        inv_n = 1.0 / jnp.minimum(pos, float(win))
        mean = wsum * jnp.concatenate([inv_n] * (POOL_GROUP_DIM // L), axis=1)
        d = (mean - pu_ref[:, lo:hi].astype(F32)).astype(BF16)
        ya = (jnp.dot(d, poolw_ref[g], preferred_element_type=F32)
              * pscale_ref[:, lo:hi] * pg_ref[:, lo:hi].astype(F32))
        u_ref[:, SSM_INNER + lo:SSM_INNER + hi] = ya.astype(BF16)
    pool_ext[L - BF16_ROWS:L, :] = pool_ext[2 * L - BF16_ROWS:2 * L, :]

    strip = 256
    for lo in range(0, SC_WIDTH, strip):
        hi = lo + strip
        cvv = ccr_ref[:, lo:hi].astype(F32) * cvr_ref[:, lo:hi].astype(F32)
        sc_ext[CONV_HALO:CONV_HALO + L, lo:hi] = cvv
        conv3 = (scw_ref[0:1, lo:hi] * sc_ext[CONV_HALO - 2:CONV_HALO - 2 + L, lo:hi]
                 + scw_ref[1:2, lo:hi] * sc_ext[CONV_HALO - 1:CONV_HALO - 1 + L, lo:hi]
                 + scw_ref[2:3, lo:hi] * cvv)
        yc = cbr_ref[:, lo:hi].astype(F32) * conv3 * cgr_ref[:, lo:hi].astype(F32)
        u0 = SSM_INNER + POOL_WIDTH
        u_ref[:, u0 + lo:u0 + hi] = yc.astype(BF16)
    sc_ext[0:CONV_HALO, :] = sc_ext[L:L + CONV_HALO, :]

    def conv_silu(cur_ref, c0, lo, hi):
        sh = jnp.dot(shift_ref[...], xbc_ext[:, c0 + lo:c0 + hi], preferred_element_type=F32)
        half = (0.5 * cbias_ref[:, c0 + lo:c0 + hi]
                + (0.5 * cw_ref[SSM_CONV - 1:SSM_CONV, c0 + lo:c0 + hi])
                * cur_ref[:, lo:hi].astype(F32))
        for k in range(1, SSM_CONV):
            tap = SSM_CONV - 1 - k
            half = half + (0.5 * cw_ref[tap:tap + 1, c0 + lo:c0 + hi]) * sh[(k - 1) * L:k * L, :]
        return half * jnp.tanh(half) + half

    n_bc = SSM_GROUPS * SSM_STATE
    xbc_ext[L:2 * L, 0:SSM_INNER] = x_ref[...]
    xbc_ext[L:2 * L, SSM_INNER:SSM_INNER + n_bc] = b_ref[...]
    xbc_ext[L:2 * L, SSM_INNER + n_bc:SSM_CONV_DIM] = c_ref[...]

    H = SSM_HEADS
    dt_t = _softplus((dt_ref[...] + dtb_ref[...]).T[0:H, :])
    a2 = -jnp.exp(alogc_ref[0:H, :]) * LOG2E
    triu = jnp.where(row <= col, 1.0, 0.0).astype(BF16)
    acum_t = _dot01_right(dt_t, triu) * a2
    a_last_col = acum_t[:, L - 1:L]
    wt_scr[...] = jnp.exp2(a_last_col - acum_t) * dt_t
    arow_scr[...] = acum_t - jnp.log2(dt_t)
    acum = jnp.concatenate([acum_t, jnp.zeros((L - H, L), F32)], axis=0).T
    acum_scr[...] = acum
    a_last_rows = _dot01_right(acum[L - BF16_ROWS:L, :], e_ref[...])
    cd_scr[...] = jnp.exp2(a_last_rows[BF16_ROWS - F32_ROWS:BF16_ROWS, :])
    left = col < SSM_HEAD_DIM

    for lo in range(0, n_bc, 256):
        bm = conv_silu(b_ref, SSM_INNER, lo, lo + 256)
        cm = conv_silu(c_ref, SSM_INNER + n_bc, lo, lo + 256)
        cbf_scr[:, lo:lo + 256] = cm.astype(BF16)
        for gg in range(2):
            bt_scr[lo // SSM_STATE + gg] = bm[:, gg * SSM_STATE:(gg + 1) * SSM_STATE].T

    for g in range(SSM_GROUPS):
        g0 = g * 512
        c_g = cbf_scr[:, g * SSM_STATE:(g + 1) * SSM_STATE]
        cb_g = jnp.dot(c_g, bt_scr[g].astype(BF16), preferred_element_type=F32)
        yoff_g = jnp.dot(c_g, state_ref[:, g0:g0 + 512].astype(BF16), preferred_element_type=F32)
        ygs = []
        for q in range(4):
            j = 4 * g + q
            lane0 = 128 * j
            if q % 2 == 0:
                xs2 = conv_silu(x_ref, 0, lane0, lane0 + 256)
            xp = xs2[:, (q % 2) * 128:(q % 2) * 128 + 128]
            lhs_parts = []
            acols = []
            for hh in range(2):
                h = 2 * j + hh
                acol = acum_scr[:, h:h + 1]
                acols.append(acol)
                decay_dt = jnp.exp2((acol + mask_scr[...]) - arow_scr[h:h + 1, :])
                m = (cb_g * decay_dt).astype(BF16)
                btw = (bt_scr[g] * wt_scr[h:h + 1, :]).astype(BF16)
                lhs_parts.append(jnp.concatenate([m, btw], axis=0))
            lhs = jnp.concatenate(lhs_parts, axis=1)
            rhs = jnp.concatenate(
                [jnp.where(left, xp, 0.0).astype(BF16),
                 jnp.where(left, 0.0, xp).astype(BF16)], axis=0)
            r = jnp.dot(lhs, rhs, preferred_element_type=F32)
            ydiag = r[0:L, :]
            st_new = r[L:2 * L, :]
            fs = jnp.exp2(jnp.where(left, acols[0], acols[1]))
            y = (ydiag + yoff_g[:, 128 * q:128 * q + 128] * fs
                 + xp * dexp_ref[:, lane0:lane0 + 128])
            ygs.append(y * z_ref[:, lane0:lane0 + 128].astype(F32))
            state_ref[:, lane0:lane0 + 128] = (
                state_ref[:, lane0:lane0 + 128]
                * cd_scr[F32_ROWS - 1:F32_ROWS, lane0:lane0 + 128] + st_new)
        yg = jnp.concatenate(ygs, axis=1)
        ms = jnp.mean(yg * yg, axis=1, keepdims=True)
        yb = yg * lax.rsqrt(ms + NORM_EPS) * nw_ref[:, g0:g0 + 512]
        u_ref[:, g0:g0 + 512] = yb.astype(BF16)
    xbc_ext[L - BF16_ROWS:L, :] = xbc_ext[2 * L - BF16_ROWS:2 * L, :]


def _mixers(proj, dtp, layer, params, cast_weights, batch, seq, chunks_per_step):
    t = proj.shape[0]
    L = CHUNK
    rows_per_step = L * chunks_per_step
    nc = seq // rows_per_step
    n_steps = batch * nc
    assert len(params) == N_MIXER_CONSTS and len(cast_weights) == N_CAST_WEIGHTS

    slab_in, slab_out, slab_shapes = [], [], []
    for w in cast_weights:
        _, w_rows, w_cols = w.shape
        slab = w_rows // n_steps
        assert slab * n_steps == w_rows and slab % BF16_ROWS == 0
        slab_in.append(pl.BlockSpec((None, slab, w_cols), lambda b, c: (layer, b * nc + c, 0)))
        slab_out.append(pl.BlockSpec((slab, w_cols), lambda b, c: (b * nc + c, 0)))
        slab_shapes.append(jax.ShapeDtypeStruct((w_rows, w_cols), BF16))
    (poolw, pscale, cw, cbias, dtb, alog, dexp, nw, scw, emat, band, shift) = params

    def rows(width, off):
        blk = off // width
        return pl.BlockSpec((rows_per_step, width), lambda b, c: (b * nc + c, blk))

    def lyr(shape):
        nd = len(shape)
        return pl.BlockSpec((None,) + shape, lambda b, c: (layer,) + (0,) * nd)

    in_specs = [
        rows(1024, OFF_PU), rows(1024, OFF_PG), rows(2048, OFF_Z), rows(2048, OFF_X),
        rows(512, OFF_B), rows(512, OFF_C),
        pl.BlockSpec((rows_per_step, DT_PAD), lambda b, c: (b * nc + c, 0)),
        rows(1024, OFF_CB), rows(1024, OFF_CC), rows(1024, OFF_CV), rows(1024, OFF_CG),
        lyr((SSM_GROUPS, POOL_GROUP_DIM, POOL_GROUP_DIM)),
        lyr((1, POOL_WIDTH)),
        lyr((SSM_CONV, SSM_CONV_DIM)),
        lyr((1, SSM_CONV_DIM)),
        lyr((1, DT_PAD)),
        lyr((SSM_HEADS, 1)),
        lyr((1, SSM_INNER)),
        lyr((1, SSM_INNER)),
        lyr((SC_KERNEL, SC_WIDTH)),
        pl.BlockSpec((DT_PAD, SSM_INNER), lambda b, c: (0, 0)),
        pl.BlockSpec((len(POOL_WINDOWS), L, 2 * L), lambda b, c: (0, 0, 0)),
        pl.BlockSpec(((SSM_CONV - 1) * L, 2 * L), lambda b, c: (0, 0)),
    ]
    ucols = SSM_INNER + POOL_WIDTH + SC_WIDTH
    assert len(in_specs) == N_MIXER_ROW_INPUTS + N_MIXER_CONSTS
    outs = pl.pallas_call(
        functools.partial(_mixer_body, chunks_per_step=chunks_per_step),
        grid=(batch, nc),
        in_specs=in_specs + slab_in,
        out_specs=[pl.BlockSpec((rows_per_step, ucols), lambda b, c: (b * nc + c, 0))] + slab_out,
        out_shape=[jax.ShapeDtypeStruct((t, ucols), BF16)] + slab_shapes,
        scratch_shapes=[
            pltpu.VMEM((2 * L, POOL_WIDTH), BF16),
            pltpu.VMEM((2 * L, SSM_CONV_DIM), BF16),
            pltpu.VMEM((CONV_HALO + L, SC_WIDTH), F32),
            pltpu.VMEM((SSM_STATE, SSM_INNER), F32),
            pltpu.VMEM((SSM_GROUPS, SSM_STATE, L), F32),
            pltpu.VMEM((L, SSM_GROUPS * SSM_STATE), BF16),
            pltpu.VMEM((L, DT_PAD), F32),
            pltpu.VMEM((SSM_HEADS, L), F32),
            pltpu.VMEM((SSM_HEADS, L), F32),
            pltpu.VMEM((F32_ROWS, SSM_INNER), F32),
            pltpu.VMEM((L, L), F32),
        ],
        compiler_params=_cparams(("arbitrary", "arbitrary")),
        name="mixers",
    )(proj, proj, proj, proj, proj, proj, dtp, proj, proj, proj, proj,
      poolw, pscale, cw, cbias, dtb, alog, dexp, nw, scw, emat, band, shift, *cast_weights)
    return outs[0], outs[1:]


def _merge_body(ub_ref, ua_ref, uc_ref, wa_ref, wb_ref, wc_ref, g0_ref, g1_ref, g2_ref, o_ref):
    ya = jnp.dot(ua_ref[...], wa_ref[...], preferred_element_type=F32)
    yb = jnp.dot(ub_ref[...], wb_ref[...], preferred_element_type=F32)
    yc = jnp.dot(uc_ref[...], wc_ref[...], preferred_element_type=F32)
    acc = (g0_ref[...].astype(F32) * ya + g1_ref[...].astype(F32) * yb
           + g2_ref[...].astype(F32) * yc)
    o_ref[...] = acc.astype(o_ref.dtype)


def _merge(u, proj, wa_bf, wb_bf, wc_bf, tm, tn):
    t = u.shape[0]
    gate_blk = OFF_GATE // tn
    per_gate = D_MODEL // tn
    return pl.pallas_call(
        _merge_body,
        grid=(D_MODEL // tn, t // tm),
        in_specs=[
            pl.BlockSpec((tm, SSM_INNER), lambda i, j: (j, 0)),
            pl.BlockSpec((tm, POOL_WIDTH), lambda i, j: (j, SSM_INNER // POOL_WIDTH)),
            pl.BlockSpec((tm, SC_WIDTH), lambda i, j: (j, (SSM_INNER + POOL_WIDTH) // SC_WIDTH)),
            pl.BlockSpec((POOL_WIDTH, tn), lambda i, j: (0, i)),
            pl.BlockSpec((SSM_INNER, tn), lambda i, j: (0, i)),
            pl.BlockSpec((SC_WIDTH, tn), lambda i, j: (0, i)),
            pl.BlockSpec((tm, tn), lambda i, j: (j, gate_blk + i)),
            pl.BlockSpec((tm, tn), lambda i, j: (j, gate_blk + per_gate + i)),
            pl.BlockSpec((tm, tn), lambda i, j: (j, gate_blk + 2 * per_gate + i)),
        ],
        out_specs=pl.BlockSpec((tm, tn), lambda i, j: (j, i)),
        out_shape=jax.ShapeDtypeStruct((t, D_MODEL), BF16),
        compiler_params=_cparams(("arbitrary", "arbitrary")),
        name="merge",
    )(u, u, u, wa_bf, wb_bf, wc_bf, proj, proj, proj)


def _out_body(m_ref, w_ref, x_ref, nw_ref, *out_refs, emit_x):
    xn = x_ref[...] + jnp.dot(m_ref[...], w_ref[...], preferred_element_type=F32)
    ms = jnp.mean(xn * xn, axis=-1, keepdims=True)
    hn = xn * lax.rsqrt(ms + NORM_EPS) * nw_ref[...]
    if emit_x:
        out_refs[0][...] = xn
        out_refs[1][...] = hn.astype(out_refs[1].dtype)
    else:
        out_refs[0][...] = hn.astype(out_refs[0].dtype)


def _out_proj(merged, wo_bf, x, nw, nw_layer, last, tm):
    t, d = x.shape
    row_spec = pl.BlockSpec((tm, d), lambda j: (j, 0))
    if last:
        out_shape = [jax.ShapeDtypeStruct((t, d), F32)]
        out_specs = [row_spec]
    else:
        out_shape = [jax.ShapeDtypeStruct((t, d), F32), jax.ShapeDtypeStruct((t, d), BF16)]
        out_specs = [row_spec, row_spec]
    return pl.pallas_call(
        functools.partial(_out_body, emit_x=not last),
        grid=(t // tm,),
        in_specs=[
            row_spec,
            pl.BlockSpec((d, d), lambda j: (0, 0)),
            row_spec,
            pl.BlockSpec((None, 1, d), lambda j: (nw_layer, 0, 0)),
        ],
        out_specs=out_specs,
        out_shape=out_shape,
        compiler_params=_cparams(("arbitrary",)),
        name="out_proj",
    )(merged, wo_bf, x, nw)


def kernel(x, norm_w, w_in, b_gate, pool_w, pool_scale, ssm_conv_w, ssm_conv_b, ssm_dt_bias,
           ssm_a_log, ssm_d, ssm_norm_w, sc_conv_w, w_br_pool, w_br_ssm, w_br_conv, w_out,
           final_norm_w):
    batch, seq, d = x.shape
    depth = w_in.shape[0]
    t = batch * seq
    assert d == D_MODEL and seq % CHUNK == 0
    tm_in = min(2048, t)
    tm_small = min(512, t)
    tn = 1024

    xf = x.reshape(t, d)

    w_in_t = jnp.swapaxes(w_in.astype(F32), 1, 2)
    bias = jnp.concatenate([jnp.zeros((depth, OFF_GATE), F32), b_gate.astype(F32)], axis=1)[:, None, :]
    wa = w_br_pool.astype(F32)
    wb = w_br_ssm.astype(F32)
    wc = w_br_conv.astype(F32)
    wo = w_out.astype(F32)

    pad_h = ((0, 0), (0, DT_PAD - SSM_HEADS))
    head_of_lane = jnp.arange(SSM_INNER, dtype=jnp.int32) // SSM_HEAD_DIM
    emat = (jnp.arange(DT_PAD, dtype=jnp.int32)[:, None] == head_of_lane[None, :]).astype(BF16)
    t_idx = jnp.arange(CHUNK, dtype=jnp.int32)[:, None] + CHUNK
    k_idx = jnp.arange(2 * CHUNK, dtype=jnp.int32)[None, :]
    band = jnp.stack([jnp.logical_and(k_idx > t_idx - win, k_idx <= t_idx) for win in POOL_WINDOWS]
                     ).astype(BF16)
    shift = jnp.concatenate([k_idx == t_idx - k for k in range(1, SSM_CONV)], axis=0).astype(BF16)
    mix_params = (
        pool_w.astype(BF16),
        pool_scale.astype(F32)[:, None, :],
        ssm_conv_w.astype(F32),
        ssm_conv_b.astype(F32)[:, None, :],
        jnp.pad(ssm_dt_bias.astype(F32), pad_h)[:, None, :],
        ssm_a_log.astype(F32)[:, :, None],
        jnp.repeat(ssm_d.astype(F32), SSM_HEAD_DIM, axis=1)[:, None, :],
        ssm_norm_w.astype(F32)[:, None, :],
        sc_conv_w.astype(F32),
        emat, band, shift,
    )
    norm_all = jnp.concatenate([norm_w.astype(F32), final_norm_w.astype(F32)[None, :]], axis=0)[:, None, :]

    h = _rmsnorm(xf, norm_all, 0, BF16, tm_small)
    for i in range(depth):
        proj = _in_proj(h, w_in_t, bias, i, tm_in, tn)
        dtp = _dt_proj(h, w_in_t, i, tm_in)
        u, (wa_bf, wb_bf, wc_bf, wo_bf) = _mixers(
            proj, dtp, i, mix_params, (wa, wb, wc, wo), batch, seq,
            2 if seq % (2 * CHUNK) == 0 else 1)
        merged = _merge(u, proj, wa_bf, wb_bf, wc_bf, tm_small, tn)
        last = i == depth - 1
        outs = _out_proj(merged, wo_bf, xf, norm_all, i + 1, last, tm_small)
        if last:
            return outs[0].reshape(batch, seq, d)
        xf, h = outs
```

```python
import functools

import jax
import jax.numpy as jnp
from jax import lax
from jax.experimental import pallas as pl
from jax.experimental.pallas import tpu as pltpu

F32 = jnp.float32
BF16 = jnp.bfloat16

NORM_EPS = 1e-6
D_MODEL = 2048
N_BRANCH = 3
POOL_WINDOWS = (2, 4, 8, 16)
POOL_WIDTH = 1024
POOL_GROUP_DIM = 256
SSM_INNER = 2048
SSM_HEAD_DIM = 64
SSM_HEADS = 32
SSM_GROUPS = 4
SSM_STATE = 128
SSM_CONV = 4
SSM_CONV_DIM = SSM_INNER + 2 * SSM_GROUPS * SSM_STATE
CHUNK = 128
SC_WIDTH = 1024
SC_KERNEL = 3

OFF_PU = 0
OFF_PG = 1024
OFF_Z = 2048
OFF_X = 4096
OFF_B = 6144
OFF_C = 6656
OFF_CB = 7168
OFF_CC = 8192
OFF_CV = 9216
OFF_CG = 10240
OFF_GATE = 11264
PROJ_COLS = OFF_GATE + N_BRANCH * D_MODEL
LANES = 128
DT_SRC = 7168
DT_PAD = LANES
DT_SHIFT = SSM_HEADS

F32_ROWS = 8
BF16_ROWS = 16
CONV_HALO = F32_ROWS
LOG2E = 1.4426950408889634
MASK_EXPONENT = -1e30

VMEM_LIMIT = 56 * 1024 * 1024


def _cparams(sem):
    return pltpu.CompilerParams(dimension_semantics=sem, vmem_limit_bytes=VMEM_LIMIT)


def _cast_rows(dst_ref, src_ref, rows_per_chunk=256):
    n = src_ref.shape[0]
    for r in range(0, n, rows_per_chunk):
        dst_ref[r:r + rows_per_chunk, :] = src_ref[r:r + rows_per_chunk, :].astype(dst_ref.dtype)


_NT = (((1,), (1,)), ((), ()))


def _rmsnorm_body(x_ref, w_ref, o_ref):
    x = x_ref[...]
    ms = jnp.mean(x * x, axis=-1, keepdims=True)
    o_ref[...] = (x * lax.rsqrt(ms + NORM_EPS) * w_ref[...]).astype(o_ref.dtype)


def _rmsnorm(x, w, layer, out_dtype, tm):
    t, d = x.shape
    return pl.pallas_call(
        _rmsnorm_body,
        grid=(t // tm,),
        in_specs=[
            pl.BlockSpec((tm, d), lambda i: (i, 0)),
            pl.BlockSpec((None, 1, d), lambda i: (layer, 0, 0)),
        ],
        out_specs=pl.BlockSpec((tm, d), lambda i: (i, 0)),
        out_shape=jax.ShapeDtypeStruct((t, d), out_dtype),
        compiler_params=_cparams(("arbitrary",)),
        name="rmsnorm",
    )(x, w)


def _dt_proj_body(h_ref, w_ref, o_ref):
    o_ref[...] = lax.dot_general(h_ref[...], w_ref[...].astype(BF16), _NT,
                                 preferred_element_type=F32)


def _dt_proj(h, w_in_t, layer, tm):
    t, k = h.shape
    return pl.pallas_call(
        _dt_proj_body,
        grid=(t // tm,),
        in_specs=[
            pl.BlockSpec((tm, k), lambda j: (j, 0)),
            pl.BlockSpec((None, DT_PAD, k), lambda j: (layer, DT_SRC // DT_PAD, 0)),
        ],
        out_specs=pl.BlockSpec((tm, DT_PAD), lambda j: (j, 0)),
        out_shape=jax.ShapeDtypeStruct((t, DT_PAD), F32),
        compiler_params=_cparams(("arbitrary",)),
        name="dt_proj",
    )(h, w_in_t)


def _in_proj_body(h_ref, w_ref, o_ref, w_scr, *, tn):
    col0 = pl.program_id(0) * tn

    @pl.when(pl.program_id(1) == 0)
    def _():
        _cast_rows(w_scr, w_ref)

    is_silu = jnp.logical_or(
        jnp.logical_and(col0 >= OFF_PG, col0 < OFF_X),
        jnp.logical_and(col0 >= OFF_CG, col0 < OFF_GATE),
    )

    def tile(silu):
        acc = lax.dot_general(h_ref[...], w_scr[...], _NT, preferred_element_type=F32)
        if silu:
            half = 0.5 * acc
            acc = half * jnp.tanh(half) + half
        o_ref[...] = acc.astype(o_ref.dtype)

    pl.when(is_silu)(functools.partial(tile, True))
    pl.when(jnp.logical_not(is_silu))(functools.partial(tile, False))


def _in_proj(h, w_in_t, layer, tm, tn):
    t, k = h.shape
    assert OFF_CB % tn == 0

    depth, n_src, _ = w_in_t.shape
    w_rows_flat = w_in_t.reshape(depth * n_src, k)

    def w_rows(i, j):
        start = i * tn
        row = layer * n_src + start + jnp.where(start >= OFF_CB, DT_SHIFT, 0)
        return (pl.multiple_of(row, DT_SHIFT), 0)

    return pl.pallas_call(
        functools.partial(_in_proj_body, tn=tn),
        grid=(PROJ_COLS // tn, t // tm),
        in_specs=[
            pl.BlockSpec((tm, k), lambda i, j: (j, 0)),
            pl.BlockSpec((pl.Element(tn), pl.Element(k)), w_rows),
        ],
        out_specs=pl.BlockSpec((tm, tn), lambda i, j: (j, i)),
        out_shape=jax.ShapeDtypeStruct((t, PROJ_COLS), BF16),
        scratch_shapes=[pltpu.VMEM((tn, k), BF16)],
        compiler_params=_cparams(("arbitrary", "arbitrary")),
        name="in_proj",
    )(h, w_rows_flat)


def _split3(v):
    hi = v.astype(BF16)
    r1 = v - hi.astype(F32)
    mid = r1.astype(BF16)
    lo = (r1 - mid.astype(F32)).astype(BF16)
    return hi, mid, lo


def _dot01_right(v, m01):
    hi, mid, lo = _split3(v)
    return (jnp.dot(hi, m01, preferred_element_type=F32)
            + jnp.dot(mid, m01, preferred_element_type=F32)
            + jnp.dot(lo, m01, preferred_element_type=F32))


def _softplus(x):
    return jnp.maximum(x, 0.0) + jnp.log1p(jnp.exp(-jnp.abs(x)))


N_MIXER_ROW_INPUTS = 11
N_MIXER_CONSTS = 12
N_CAST_WEIGHTS = 4


def _mixer_body(*refs, chunks_per_step):
    n_in = N_MIXER_ROW_INPUTS + N_MIXER_CONSTS
    row_refs = refs[:N_MIXER_ROW_INPUTS]
    consts = refs[N_MIXER_ROW_INPUTS:n_in]
    w_f32_refs = refs[n_in:n_in + N_CAST_WEIGHTS]
    u_ref = refs[n_in + N_CAST_WEIGHTS]
    w_bf16_refs = refs[n_in + N_CAST_WEIGHTS + 1:n_in + 2 * N_CAST_WEIGHTS + 1]
    scratch = refs[n_in + 2 * N_CAST_WEIGHTS + 1:]
    for src, dst in zip(w_f32_refs, w_bf16_refs):
        dst[...] = src[...].astype(dst.dtype)
    for sub in range(chunks_per_step):
        rows = pl.ds(sub * CHUNK, CHUNK)
        _mixer_chunk(pl.program_id(1) * chunks_per_step + sub, sub == 0,
                     *[r.at[rows] for r in row_refs], *consts, u_ref.at[rows], *scratch)


def _mixer_chunk(c, may_start_sequence, pu_ref, pg_ref, z_ref, x_ref, b_ref, c_ref, dt_ref,
                 cbr_ref, ccr_ref, cvr_ref, cgr_ref,
                 poolw_ref, pscale_ref, cw_ref, cbias_ref, dtb_ref, alogc_ref,
                 dexp_ref, nw_ref, scw_ref, e_ref, band_ref, shift_ref,
                 u_ref,
                 pool_ext, xbc_ext, sc_ext, state_ref,
                 bt_scr, cbf_scr, acum_scr, arow_scr, wt_scr, cd_scr, mask_scr):
    L = CHUNK
    row = lax.broadcasted_iota(jnp.int32, (L, L), 0)
    col = lax.broadcasted_iota(jnp.int32, (L, L), 1)

    if may_start_sequence:
        @pl.when(c == 0)
        def _():
            pool_ext[0:L, :] = jnp.zeros((L, POOL_WIDTH), BF16)
            xbc_ext[0:L, :] = jnp.zeros((L, SSM_CONV_DIM), BF16)
            sc_ext[0:CONV_HALO, :] = jnp.zeros((CONV_HALO, SC_WIDTH), F32)
            state_ref[...] = jnp.zeros_like(state_ref)
            mask_scr[...] = jnp.where(row >= col, 0.0, MASK_EXPONENT)

    pool_ext[L:2 * L, :] = pu_ref[...]
    pos = (c * L + 1 + row).astype(F32)
    for g, win in enumerate(POOL_WINDOWS):
        lo, hi = g * POOL_GROUP_DIM, (g + 1) * POOL_GROUP_DIM
        wsum = jnp.dot(band_ref[g], pool_ext[:, lo:hi], preferred_element_type=F32)
        inv_n = 1.0 / jnp.minimum(pos, float(win))
        mean = wsum * jnp.concatenate([inv_n] * (POOL_GROUP_DIM // L), axis=1)
        d = (mean - pu_ref[:, lo:hi].astype(F32)).astype(BF16)
        ya = (jnp.dot(d, poolw_ref[g], preferred_element_type=F32)
              * pscale_ref[:, lo:hi] * pg_ref[:, lo:hi].astype(F32))
        u_ref[:, SSM_INNER + lo:SSM_INNER + hi] = ya.astype(BF16)
    pool_ext[L - BF16_ROWS:L, :] = pool_ext[2 * L - BF16_ROWS:2 * L, :]

    strip = 256
    for lo in range(0, SC_WIDTH, strip):
        hi = lo + strip
        cvv = ccr_ref[:, lo:hi].astype(F32) * cvr_ref[:, lo:hi].astype(F32)
        sc_ext[CONV_HALO:CONV_HALO + L, lo:hi] = cvv
        conv3 = (scw_ref[0:1, lo:hi] * sc_ext[CONV_HALO - 2:CONV_HALO - 2 + L, lo:hi]
                 + scw_ref[1:2, lo:hi] * sc_ext[CONV_HALO - 1:CONV_HALO - 1 + L, lo:hi]
                 + scw_ref[2:3, lo:hi] * cvv)
        yc = cbr_ref[:, lo:hi].astype(F32) * conv3 * cgr_ref[:, lo:hi].astype(F32)
        u0 = SSM_INNER + POOL_WIDTH
        u_ref[:, u0 + lo:u0 + hi] = yc.astype(BF16)
    sc_ext[0:CONV_HALO, :] = sc_ext[L:L + CONV_HALO, :]

    def conv_silu(cur_ref, c0, lo, hi):
        sh = jnp.dot(shift_ref[...], xbc_ext[:, c0 + lo:c0 + hi], preferred_element_type=F32)
        half = (0.5 * cbias_ref[:, c0 + lo:c0 + hi]
                + (0.5 * cw_ref[SSM_CONV - 1:SSM_CONV, c0 + lo:c0 + hi])
                * cur_ref[:, lo:hi].astype(F32))
        for k in range(1, SSM_CONV):
            tap = SSM_CONV - 1 - k
            half = half + (0.5 * cw_ref[tap:tap + 1, c0 + lo:c0 + hi]) * sh[(k - 1) * L:k * L, :]
        return half * jnp.tanh(half) + half

    n_bc = SSM_GROUPS * SSM_STATE
    xbc_ext[L:2 * L, 0:SSM_INNER] = x_ref[...]
    xbc_ext[L:2 * L, SSM_INNER:SSM_INNER + n_bc] = b_ref[...]
    xbc_ext[L:2 * L, SSM_INNER + n_bc:SSM_CONV_DIM] = c_ref[...]

    H = SSM_HEADS
    dt_t = _softplus((dt_ref[...] + dtb_ref[...]).T[0:H, :])
    a2 = -jnp.exp(alogc_ref[0:H, :]) * LOG2E
    triu = jnp.where(row <= col, 1.0, 0.0).astype(BF16)
    acum_t = _dot01_right(dt_t, triu) * a2
    a_last_col = acum_t[:, L - 1:L]
    wt_scr[...] = jnp.exp2(a_last_col - acum_t) * dt_t
    arow_scr[...] = acum_t - jnp.log2(dt_t)
    acum = jnp.concatenate([acum_t, jnp.zeros((L - H, L), F32)], axis=0).T
    acum_scr[...] = acum
    a_last_rows = _dot01_right(acum[L - BF16_ROWS:L, :], e_ref[...])
    cd_scr[...] = jnp.exp2(a_last_rows[BF16_ROWS - F32_ROWS:BF16_ROWS, :])
    left = col < SSM_HEAD_DIM

    for lo in range(0, n_bc, 256):
        bm = conv_silu(b_ref, SSM_INNER, lo, lo + 256)
        cm = conv_silu(c_ref, SSM_INNER + n_bc, lo, lo + 256)
        cbf_scr[:, lo:lo + 256] = cm.astype(BF16)
        for gg in range(2):
            bt_scr[lo // SSM_STATE + gg] = bm[:, gg * SSM_STATE:(gg + 1) * SSM_STATE].T

    for g in range(SSM_GROUPS):
        g0 = g * 512
        c_g = cbf_scr[:, g * SSM_STATE:(g + 1) * SSM_STATE]
        cb_g = jnp.dot(c_g, bt_scr[g].astype(BF16), preferred_element_type=F32)
        yoff_g = jnp.dot(c_g, state_ref[:, g0:g0 + 512].astype(BF16), preferred_element_type=F32)
        ygs = []
        for q in range(4):
            j = 4 * g + q
            lane0 = 128 * j
            if q % 2 == 0:
                xs2 = conv_silu(x_ref, 0, lane0, lane0 + 256)
            xp = xs2[:, (q % 2) * 128:(q % 2) * 128 + 128]
            lhs_parts = []
            acols = []
            for hh in range(2):
                h = 2 * j + hh
                acol = acum_scr[:, h:h + 1]
                acols.append(acol)
                decay_dt = jnp.exp2((acol + mask_scr[...]) - arow_scr[h:h + 1, :])
                m = (cb_g * decay_dt).astype(BF16)
                btw = (bt_scr[g] * wt_scr[h:h + 1, :]).astype(BF16)
                lhs_parts.append(jnp.concatenate([m, btw], axis=0))
            lhs = jnp.concatenate(lhs_parts, axis=1)
            rhs = jnp.concatenate(
                [jnp.where(left, xp, 0.0).astype(BF16),
                 jnp.where(left, 0.0, xp).astype(BF16)], axis=0)
            r = jnp.dot(lhs, rhs, preferred_element_type=F32)
            ydiag = r[0:L, :]
            st_new = r[L:2 * L, :]
            fs = jnp.exp2(jnp.where(left, acols[0], acols[1]))
            y = (ydiag + yoff_g[:, 128 * q:128 * q + 128] * fs
                 + xp * dexp_ref[:, lane0:lane0 + 128])
            ygs.append(y * z_ref[:, lane0:lane0 + 128].astype(F32))
            state_ref[:, lane0:lane0 + 128] = (
                state_ref[:, lane0:lane0 + 128]
                * cd_scr[F32_ROWS - 1:F32_ROWS, lane0:lane0 + 128] + st_new)
        yg = jnp.concatenate(ygs, axis=1)
        ms = jnp.mean(yg * yg, axis=1, keepdims=True)
        yb = yg * lax.rsqrt(ms + NORM_EPS) * nw_ref[:, g0:g0 + 512]
        u_ref[:, g0:g0 + 512] = yb.astype(BF16)
    xbc_ext[L - BF16_ROWS:L, :] = xbc_ext[2 * L - BF16_ROWS:2 * L, :]


def _mixers(proj, dtp, layer, params, cast_weights, batch, seq, chunks_per_step):
    t = proj.shape[0]
    L = CHUNK
    rows_per_step = L * chunks_per_step
    nc = seq // rows_per_step
    n_steps = batch * nc
    assert len(params) == N_MIXER_CONSTS and len(cast_weights) == N_CAST_WEIGHTS

    slab_in, slab_out, slab_shapes = [], [], []
    for w in cast_weights:
        _, w_rows, w_cols = w.shape
        slab = w_rows // n_steps
        assert slab * n_steps == w_rows and slab % BF16_ROWS == 0
        slab_in.append(pl.BlockSpec((None, slab, w_cols), lambda b, c: (layer, b * nc + c, 0)))
        slab_out.append(pl.BlockSpec((slab, w_cols), lambda b, c: (b * nc + c, 0)))
        slab_shapes.append(jax.ShapeDtypeStruct((w_rows, w_cols), BF16))
    (poolw, pscale, cw, cbias, dtb, alog, dexp, nw, scw, emat, band, shift) = params

    def rows(width, off):
        blk = off // width
        return pl.BlockSpec((rows_per_step, width), lambda b, c: (b * nc + c, blk))

    def lyr(shape):
        nd = len(shape)
        return pl.BlockSpec((None,) + shape, lambda b, c: (layer,) + (0,) * nd)

    in_specs = [
        rows(1024, OFF_PU), rows(1024, OFF_PG), rows(2048, OFF_Z), rows(2048, OFF_X),
        rows(512, OFF_B), rows(512, OFF_C),
        pl.BlockSpec((rows_per_step, DT_PAD), lambda b, c: (b * nc + c, 0)),
        rows(1024, OFF_CB), rows(1024, OFF_CC), rows(1024, OFF_CV), rows(1024, OFF_CG),
        lyr((SSM_GROUPS, POOL_GROUP_DIM, POOL_GROUP_DIM)),
        lyr((1, POOL_WIDTH)),
        lyr((SSM_CONV, SSM_CONV_DIM)),
        lyr((1, SSM_CONV_DIM)),
        lyr((1, DT_PAD)),
        lyr((SSM_HEADS, 1)),
        lyr((1, SSM_INNER)),
        lyr((1, SSM_INNER)),
        lyr((SC_KERNEL, SC_WIDTH)),
        pl.BlockSpec((DT_PAD, SSM_INNER), lambda b, c: (0, 0)),
        pl.BlockSpec((len(POOL_WINDOWS), L, 2 * L), lambda b, c: (0, 0, 0)),
        pl.BlockSpec(((SSM_CONV - 1) * L, 2 * L), lambda b, c: (0, 0)),
    ]
    ucols = SSM_INNER + POOL_WIDTH + SC_WIDTH
    assert len(in_specs) == N_MIXER_ROW_INPUTS + N_MIXER_CONSTS
    outs = pl.pallas_call(
        functools.partial(_mixer_body, chunks_per_step=chunks_per_step),
        grid=(batch, nc),
        in_specs=in_specs + slab_in,
        out_specs=[pl.BlockSpec((rows_per_step, ucols), lambda b, c: (b * nc + c, 0))] + slab_out,
        out_shape=[jax.ShapeDtypeStruct((t, ucols), BF16)] + slab_shapes,
        scratch_shapes=[
            pltpu.VMEM((2 * L, POOL_WIDTH), BF16),
            pltpu.VMEM((2 * L, SSM_CONV_DIM), BF16),
            pltpu.VMEM((CONV_HALO + L, SC_WIDTH), F32),
            pltpu.VMEM((SSM_STATE, SSM_INNER), F32),
            pltpu.VMEM((SSM_GROUPS, SSM_STATE, L), F32),
            pltpu.VMEM((L, SSM_GROUPS * SSM_STATE), BF16),
            pltpu.VMEM((L, DT_PAD), F32),
            pltpu.VMEM((SSM_HEADS, L), F32),
            pltpu.VMEM((SSM_HEADS, L), F32),
            pltpu.VMEM((F32_ROWS, SSM_INNER), F32),
            pltpu.VMEM((L, L), F32),
        ],
        compiler_params=_cparams(("arbitrary", "arbitrary")),
        name="mixers",
    )(proj, proj, proj, proj, proj, proj, dtp, proj, proj, proj, proj,
      poolw, pscale, cw, cbias, dtb, alog, dexp, nw, scw, emat, band, shift, *cast_weights)
    return outs[0], outs[1:]


def _merge_body(ub_ref, ua_ref, uc_ref, wa_ref, wb_ref, wc_ref, g0_ref, g1_ref, g2_ref,
                b0_ref, b1_ref, b2_ref, o_ref):
    def gate(g_ref, b_ref):
        return 0.5 * jnp.tanh(0.5 * (g_ref[...].astype(F32) + b_ref[...])) + 0.5

    ya = jnp.dot(ua_ref[...], wa_ref[...], preferred_element_type=F32)
    yb = jnp.dot(ub_ref[...], wb_ref[...], preferred_element_type=F32)
    yc = jnp.dot(uc_ref[...], wc_ref[...], preferred_element_type=F32)
    acc = gate(g0_ref, b0_ref) * ya + gate(g1_ref, b1_ref) * yb + gate(g2_ref, b2_ref) * yc
    o_ref[...] = acc.astype(o_ref.dtype)


def _merge(u, proj, b_gate, wa_bf, wb_bf, wc_bf, layer, tm, tn):
    t = u.shape[0]
    gate_blk = OFF_GATE // tn
    per_gate = D_MODEL // tn

    def bias(k):
        return pl.BlockSpec((None, 1, tn), lambda i, j: (layer, 0, k * per_gate + i))
    return pl.pallas_call(
        _merge_body,
        grid=(D_MODEL // tn, t // tm),
        in_specs=[
            pl.BlockSpec((tm, SSM_INNER), lambda i, j: (j, 0)),
            pl.BlockSpec((tm, POOL_WIDTH), lambda i, j: (j, SSM_INNER // POOL_WIDTH)),
            pl.BlockSpec((tm, SC_WIDTH), lambda i, j: (j, (SSM_INNER + POOL_WIDTH) // SC_WIDTH)),
            pl.BlockSpec((POOL_WIDTH, tn), lambda i, j: (0, i)),
            pl.BlockSpec((SSM_INNER, tn), lambda i, j: (0, i)),
            pl.BlockSpec((SC_WIDTH, tn), lambda i, j: (0, i)),
            pl.BlockSpec((tm, tn), lambda i, j: (j, gate_blk + i)),
            pl.BlockSpec((tm, tn), lambda i, j: (j, gate_blk + per_gate + i)),
            pl.BlockSpec((tm, tn), lambda i, j: (j, gate_blk + 2 * per_gate + i)),
            bias(0), bias(1), bias(2),
        ],
        out_specs=pl.BlockSpec((tm, tn), lambda i, j: (j, i)),
        out_shape=jax.ShapeDtypeStruct((t, D_MODEL), BF16),
        compiler_params=_cparams(("arbitrary", "arbitrary")),
        name="merge",
    )(u, u, u, wa_bf, wb_bf, wc_bf, proj, proj, proj, b_gate, b_gate, b_gate)


def _out_body(m_ref, w_ref, x_ref, nw_ref, *out_refs, emit_x):
    xn = x_ref[...] + jnp.dot(m_ref[...], w_ref[...], preferred_element_type=F32)
    ms = jnp.mean(xn * xn, axis=-1, keepdims=True)
    hn = xn * lax.rsqrt(ms + NORM_EPS) * nw_ref[...]
    if emit_x:
        out_refs[0][...] = xn
        out_refs[1][...] = hn.astype(out_refs[1].dtype)
    else:
        out_refs[0][...] = hn.astype(out_refs[0].dtype)


def _out_proj(merged, wo_bf, x, nw, nw_layer, last, tm):
    t, d = x.shape
    row_spec = pl.BlockSpec((tm, d), lambda j: (j, 0))
    if last:
        out_shape = [jax.ShapeDtypeStruct((t, d), F32)]
        out_specs = [row_spec]
    else:
        out_shape = [jax.ShapeDtypeStruct((t, d), F32), jax.ShapeDtypeStruct((t, d), BF16)]
        out_specs = [row_spec, row_spec]
    return pl.pallas_call(
        functools.partial(_out_body, emit_x=not last),
        grid=(t // tm,),
        in_specs=[
            row_spec,
            pl.BlockSpec((d, d), lambda j: (0, 0)),
            row_spec,
            pl.BlockSpec((None, 1, d), lambda j: (nw_layer, 0, 0)),
        ],
        out_specs=out_specs,
        out_shape=out_shape,
        compiler_params=_cparams(("arbitrary",)),
        name="out_proj",
    )(merged, wo_bf, x, nw)


def kernel(x, norm_w, w_in, b_gate, pool_w, pool_scale, ssm_conv_w, ssm_conv_b, ssm_dt_bias,
           ssm_a_log, ssm_d, ssm_norm_w, sc_conv_w, w_br_pool, w_br_ssm, w_br_conv, w_out,
           final_norm_w):
    batch, seq, d = x.shape
    depth = w_in.shape[0]
    t = batch * seq
    assert d == D_MODEL and seq % CHUNK == 0
    tm_in = min(2048, t)
    tm_small = min(512, t)
    tn = 1024

    xf = x.reshape(t, d)

    w_in_t = jnp.swapaxes(w_in.astype(F32), 1, 2)
    gate_bias = b_gate.astype(F32)[:, None, :]
    wa = w_br_pool.astype(F32)
    wb = w_br_ssm.astype(F32)
    wc = w_br_conv.astype(F32)
    wo = w_out.astype(F32)

    pad_h = ((0, 0), (0, DT_PAD - SSM_HEADS))
    head_of_lane = jnp.arange(SSM_INNER, dtype=jnp.int32) // SSM_HEAD_DIM
    emat = (jnp.arange(DT_PAD, dtype=jnp.int32)[:, None] == head_of_lane[None, :]).astype(BF16)
    t_idx = jnp.arange(CHUNK, dtype=jnp.int32)[:, None] + CHUNK
    k_idx = jnp.arange(2 * CHUNK, dtype=jnp.int32)[None, :]
    band = jnp.stack([jnp.logical_and(k_idx > t_idx - win, k_idx <= t_idx) for win in POOL_WINDOWS]
                     ).astype(BF16)
    shift = jnp.concatenate([k_idx == t_idx - k for k in range(1, SSM_CONV)], axis=0).astype(BF16)
    mix_params = (
        pool_w.astype(BF16),
        pool_scale.astype(F32)[:, None, :],
        ssm_conv_w.astype(F32),
        ssm_conv_b.astype(F32)[:, None, :],
        jnp.pad(ssm_dt_bias.astype(F32), pad_h)[:, None, :],
        ssm_a_log.astype(F32)[:, :, None],
        jnp.repeat(ssm_d.astype(F32), SSM_HEAD_DIM, axis=1)[:, None, :],
        ssm_norm_w.astype(F32)[:, None, :],
        sc_conv_w.astype(F32),
        emat, band, shift,
    )
    norm_all = jnp.concatenate([norm_w.astype(F32), final_norm_w.astype(F32)[None, :]], axis=0)[:, None, :]

    h = _rmsnorm(xf, norm_all, 0, BF16, tm_small)
    for i in range(depth):
        proj = _in_proj(h, w_in_t, i, tm_in, tn)
        dtp = _dt_proj(h, w_in_t, i, tm_in)
        u, (wa_bf, wb_bf, wc_bf, wo_bf) = _mixers(
            proj, dtp, i, mix_params, (wa, wb, wc, wo), batch, seq,
            2 if seq % (2 * CHUNK) == 0 else 1)
        merged = _merge(u, proj, gate_bias, wa_bf, wb_bf, wc_bf, i, tm_small, tn)
        last = i == depth - 1
        outs = _out_proj(merged, wo_bf, xf, norm_all, i + 1, last, tm_small)
        if last:
            return outs[0].reshape(batch, seq, d)
        xf, h = outs
```

```python
import functools

import jax
import jax.numpy as jnp
from jax import lax
from jax.experimental import pallas as pl
from jax.experimental.pallas import tpu as pltpu

F32 = jnp.float32
BF16 = jnp.bfloat16

NORM_EPS = 1e-6
D_MODEL = 2048
N_BRANCH = 3
POOL_WINDOWS = (2, 4, 8, 16)
POOL_WIDTH = 1024
POOL_GROUP_DIM = 256
SSM_INNER = 2048
SSM_HEAD_DIM = 64
SSM_HEADS = 32
SSM_GROUPS = 4
SSM_STATE = 128
SSM_CONV = 4
SSM_CONV_DIM = SSM_INNER + 2 * SSM_GROUPS * SSM_STATE
CHUNK = 128
SC_WIDTH = 1024
SC_KERNEL = 3

OFF_PU = 0
OFF_PG = 1024
OFF_Z = 2048
OFF_X = 4096
OFF_B = 6144
OFF_C = 6656
OFF_CB = 7168
OFF_CC = 8192
OFF_CV = 9216
OFF_CG = 10240
OFF_GATE = 11264
PROJ_COLS = OFF_GATE + N_BRANCH * D_MODEL
LANES = 128
DT_SRC = 7168
DT_PAD = LANES
DT_SHIFT = SSM_HEADS

F32_ROWS = 8
BF16_ROWS = 16
CONV_HALO = F32_ROWS
LOG2E = 1.4426950408889634
MASK_EXPONENT = -1e30

VMEM_LIMIT = 56 * 1024 * 1024


def _cparams(sem):
    return pltpu.CompilerParams(dimension_semantics=sem, vmem_limit_bytes=VMEM_LIMIT)


def _cast_rows(dst_ref, src_ref, rows_per_chunk=256):
    n = src_ref.shape[0]
    for r in range(0, n, rows_per_chunk):
        dst_ref[r:r + rows_per_chunk, :] = src_ref[r:r + rows_per_chunk, :].astype(dst_ref.dtype)


_NT = (((1,), (1,)), ((), ()))


def _rmsnorm_body(x_ref, w_ref, o_ref):
    x = x_ref[...]
    ms = jnp.mean(x * x, axis=-1, keepdims=True)
    o_ref[...] = (x * lax.rsqrt(ms + NORM_EPS) * w_ref[...]).astype(o_ref.dtype)


def _rmsnorm(x, w, layer, out_dtype, tm):
    t, d = x.shape
    return pl.pallas_call(
        _rmsnorm_body,
        grid=(t // tm,),
        in_specs=[
            pl.BlockSpec((tm, d), lambda i: (i, 0)),
            pl.BlockSpec((None, 1, d), lambda i: (layer, 0, 0)),
        ],
        out_specs=pl.BlockSpec((tm, d), lambda i: (i, 0)),
        out_shape=jax.ShapeDtypeStruct((t, d), out_dtype),
        compiler_params=_cparams(("arbitrary",)),
        name="rmsnorm",
    )(x, w)


def _dt_proj_body(h_ref, w_ref, o_ref):
    o_ref[...] = lax.dot_general(h_ref[...], w_ref[...].astype(BF16), _NT,
                                 preferred_element_type=F32)


def _dt_proj(h, w_in_t, layer, tm):
    t, k = h.shape
    return pl.pallas_call(
        _dt_proj_body,
        grid=(t // tm,),
        in_specs=[
            pl.BlockSpec((tm, k), lambda j: (j, 0)),
            pl.BlockSpec((None, DT_PAD, k), lambda j: (layer, DT_SRC // DT_PAD, 0)),
        ],
        out_specs=pl.BlockSpec((tm, DT_PAD), lambda j: (j, 0)),
        out_shape=jax.ShapeDtypeStruct((t, DT_PAD), F32),
        compiler_params=_cparams(("arbitrary",)),
        name="dt_proj",
    )(h, w_in_t)


def _in_proj_body(h_ref, w_ref, o_ref, w_scr, *, tn):
    col0 = pl.program_id(0) * tn

    @pl.when(pl.program_id(1) == 0)
    def _():
        _cast_rows(w_scr, w_ref)

    is_silu = jnp.logical_or(
        jnp.logical_and(col0 >= OFF_PG, col0 < OFF_X),
        jnp.logical_and(col0 >= OFF_CG, col0 < OFF_GATE),
    )

    def tile(silu):
        acc = lax.dot_general(h_ref[...], w_scr[...], _NT, preferred_element_type=F32)
        if silu:
            half = 0.5 * acc
            acc = half * jnp.tanh(half) + half
        o_ref[...] = acc.astype(o_ref.dtype)

    pl.when(is_silu)(functools.partial(tile, True))
    pl.when(jnp.logical_not(is_silu))(functools.partial(tile, False))


def _in_proj(h, w_in_t, layer, tm, tn):
    t, k = h.shape
    assert OFF_CB % tn == 0

    depth, n_src, _ = w_in_t.shape
    w_rows_flat = w_in_t.reshape(depth * n_src, k)

    def w_rows(i, j):
        start = i * tn
        row = layer * n_src + start + jnp.where(start >= OFF_CB, DT_SHIFT, 0)
        return (pl.multiple_of(row, DT_SHIFT), 0)

    return pl.pallas_call(
        functools.partial(_in_proj_body, tn=tn),
        grid=(PROJ_COLS // tn, t // tm),
        in_specs=[
            pl.BlockSpec((tm, k), lambda i, j: (j, 0)),
            pl.BlockSpec((pl.Element(tn), pl.Element(k)), w_rows),
        ],
        out_specs=pl.BlockSpec((tm, tn), lambda i, j: (j, i)),
        out_shape=jax.ShapeDtypeStruct((t, PROJ_COLS), BF16),
        scratch_shapes=[pltpu.VMEM((tn, k), BF16)],
        compiler_params=_cparams(("arbitrary", "arbitrary")),
        name="in_proj",
    )(h, w_rows_flat)


def _split3(v):
    hi = v.astype(BF16)
    r1 = v - hi.astype(F32)
    mid = r1.astype(BF16)
    lo = (r1 - mid.astype(F32)).astype(BF16)
    return hi, mid, lo


def _dot01_right(v, m01):
    hi, mid, lo = _split3(v)
    return (jnp.dot(hi, m01, preferred_element_type=F32)
            + jnp.dot(mid, m01, preferred_element_type=F32)
            + jnp.dot(lo, m01, preferred_element_type=F32))


def _softplus(x):
    return jnp.maximum(x, 0.0) + jnp.log1p(jnp.exp(-jnp.abs(x)))


N_MIXER_ROW_INPUTS = 11
N_MIXER_CONSTS = 12
N_CAST_WEIGHTS = 4


def _mixer_body(*refs, chunks_per_step):
    n_in = N_MIXER_ROW_INPUTS + N_MIXER_CONSTS
    row_refs = refs[:N_MIXER_ROW_INPUTS]
    consts = refs[N_MIXER_ROW_INPUTS:n_in]
    w_f32_refs = refs[n_in:n_in + N_CAST_WEIGHTS]
    u_ref = refs[n_in + N_CAST_WEIGHTS]
    w_bf16_refs = refs[n_in + N_CAST_WEIGHTS + 1:n_in + 2 * N_CAST_WEIGHTS + 1]
    scratch = refs[n_in + 2 * N_CAST_WEIGHTS + 1:]
    for src, dst in zip(w_f32_refs, w_bf16_refs):
        dst[...] = src[...].astype(dst.dtype)
    for sub in range(chunks_per_step):
        rows = pl.ds(sub * CHUNK, CHUNK)
        _mixer_chunk(pl.program_id(1) * chunks_per_step + sub, sub == 0,
                     *[r.at[rows] for r in row_refs], *consts, u_ref.at[rows], *scratch)


def _mixer_chunk(c, may_start_sequence, pu_ref, pg_ref, z_ref, x_ref, b_ref, c_ref, dt_ref,
                 cbr_ref, ccr_ref, cvr_ref, cgr_ref,
                 poolw_ref, pscale_ref, cw_ref, cbias_ref, dtb_ref, alogc_ref,
                 dexp_ref, nw_ref, scw_ref, e_ref, band_ref, shift_ref,
                 u_ref,
                 pool_ext, xbc_ext, sc_ext, state_ref,
                 bt_scr, cbf_scr, acum_scr, arow_scr, wt_scr, cd_scr, mask_scr):
    L = CHUNK
    row = lax.broadcasted_iota(jnp.int32, (L, L), 0)
    col = lax.broadcasted_iota(jnp.int32, (L, L), 1)

    if may_start_sequence:
        @pl.when(c == 0)
        def _():
            pool_ext[0:L, :] = jnp.zeros((L, POOL_WIDTH), BF16)
            xbc_ext[0:L, :] = jnp.zeros((L, SSM_CONV_DIM), BF16)
            sc_ext[0:CONV_HALO, :] = jnp.zeros((CONV_HALO, SC_WIDTH), F32)
            state_ref[...] = jnp.zeros_like(state_ref)
            mask_scr[...] = jnp.where(row >= col, 0.0, MASK_EXPONENT)

    pool_ext[L:2 * L, :] = pu_ref[...]
    pos = (c * L + 1 + row).astype(F32)
    for g, win in enumerate(POOL_WINDOWS):
        lo, hi = g * POOL_GROUP_DIM, (g + 1) * POOL_GROUP_DIM
        wsum = jnp.dot(band_ref[g], pool_ext[:, lo:hi], preferred_element_type=F32)
        inv_n = 1.0 / jnp.minimum(pos, float(win))
        mean = wsum * jnp.concatenate([inv_n] * (POOL_GROUP_DIM // L), axis=1)
        d = (mean - pu_ref[:, lo:hi].astype(F32)).astype(BF16)
        ya = (jnp.dot(d, poolw_ref[g], preferred_element_type=F32)
              * pscale_ref[:, lo:hi] * pg_ref[:, lo:hi].astype(F32))
        u_ref[:, SSM_INNER + lo:SSM_INNER + hi] = ya.astype(BF16)
    pool_ext[L - BF16_ROWS:L, :] = pool_ext[2 * L - BF16_ROWS:2 * L, :]

    strip = 256
    for lo in range(0, SC_WIDTH, strip):
        hi = lo + strip
        cvv = ccr_ref[:, lo:hi].astype(F32) * cvr_ref[:, lo:hi].astype(F32)
        sc_ext[CONV_HALO:CONV_HALO + L, lo:hi] = cvv
        conv3 = (scw_ref[0:1, lo:hi] * sc_ext[CONV_HALO - 2:CONV_HALO - 2 + L, lo:hi]
                 + scw_ref[1:2, lo:hi] * sc_ext[CONV_HALO - 1:CONV_HALO - 1 + L, lo:hi]
                 + scw_ref[2:3, lo:hi] * cvv)
        yc = cbr_ref[:, lo:hi].astype(F32) * conv3 * cgr_ref[:, lo:hi].astype(F32)
        u0 = SSM_INNER + POOL_WIDTH
        u_ref[:, u0 + lo:u0 + hi] = yc.astype(BF16)
    sc_ext[0:CONV_HALO, :] = sc_ext[L:L + CONV_HALO, :]

    def conv_silu(cur_ref, c0, lo, hi):
        sh = jnp.dot(shift_ref[...], xbc_ext[:, c0 + lo:c0 + hi], preferred_element_type=F32)
        half = (0.5 * cbias_ref[:, c0 + lo:c0 + hi]
                + (0.5 * cw_ref[SSM_CONV - 1:SSM_CONV, c0 + lo:c0 + hi])
                * cur_ref[:, lo:hi].astype(F32))
        for k in range(1, SSM_CONV):
            tap = SSM_CONV - 1 - k
            half = half + (0.5 * cw_ref[tap:tap + 1, c0 + lo:c0 + hi]) * sh[(k - 1) * L:k * L, :]
        return half * jnp.tanh(half) + half

    n_bc = SSM_GROUPS * SSM_STATE
    xbc_ext[L:2 * L, 0:SSM_INNER] = x_ref[...]
    xbc_ext[L:2 * L, SSM_INNER:SSM_INNER + n_bc] = b_ref[...]
    xbc_ext[L:2 * L, SSM_INNER + n_bc:SSM_CONV_DIM] = c_ref[...]

    H = SSM_HEADS
    dt_t = _softplus((dt_ref[...] + dtb_ref[...]).T[0:H, :])
    a2 = -jnp.exp(alogc_ref[0:H, :]) * LOG2E
    triu = jnp.where(row <= col, 1.0, 0.0).astype(BF16)
    acum_t = _dot01_right(dt_t, triu) * a2
    a_last_col = acum_t[:, L - 1:L]
    wt_scr[...] = jnp.exp2(a_last_col - acum_t) * dt_t
    arow_scr[...] = acum_t - jnp.log2(dt_t)
    acum = jnp.concatenate([acum_t, jnp.zeros((L - H, L), F32)], axis=0).T
    acum_scr[...] = acum
    a_last_rows = _dot01_right(acum[L - BF16_ROWS:L, :], e_ref[...])
    cd_scr[...] = jnp.exp2(a_last_rows[BF16_ROWS - F32_ROWS:BF16_ROWS, :])
    left = col < SSM_HEAD_DIM

    for lo in range(0, n_bc, 256):
        bm = conv_silu(b_ref, SSM_INNER, lo, lo + 256)
        cm = conv_silu(c_ref, SSM_INNER + n_bc, lo, lo + 256)
        cbf_scr[:, lo:lo + 256] = cm.astype(BF16)
        for gg in range(2):
            bt_scr[lo // SSM_STATE + gg] = bm[:, gg * SSM_STATE:(gg + 1) * SSM_STATE].T

    for g in range(SSM_GROUPS):
        g0 = g * 512
        c_g = cbf_scr[:, g * SSM_STATE:(g + 1) * SSM_STATE]
        cb_g = jnp.dot(c_g, bt_scr[g].astype(BF16), preferred_element_type=F32)
        yoff_g = jnp.dot(c_g, state_ref[:, g0:g0 + 512].astype(BF16), preferred_element_type=F32)
        ygs = []
        for q in range(4):
            j = 4 * g + q
            lane0 = 128 * j
            if q % 2 == 0:
                xs2 = conv_silu(x_ref, 0, lane0, lane0 + 256)
            xp = xs2[:, (q % 2) * 128:(q % 2) * 128 + 128]
            lhs_parts = []
            acols = []
            for hh in range(2):
                h = 2 * j + hh
                acol = acum_scr[:, h:h + 1]
                acols.append(acol)
                decay_dt = jnp.exp2((acol + mask_scr[...]) - arow_scr[h:h + 1, :])
                m = (cb_g * decay_dt).astype(BF16)
                btw = (bt_scr[g] * wt_scr[h:h + 1, :]).astype(BF16)
                lhs_parts.append(jnp.concatenate([m, btw], axis=0))
            lhs = jnp.concatenate(lhs_parts, axis=1)
            rhs = jnp.concatenate(
                [jnp.where(left, xp, 0.0).astype(BF16),
                 jnp.where(left, 0.0, xp).astype(BF16)], axis=0)
            r = jnp.dot(lhs, rhs, preferred_element_type=F32)
            ydiag = r[0:L, :]
            st_new = r[L:2 * L, :]
            fs = jnp.exp2(jnp.where(left, acols[0], acols[1]))
            y = (ydiag + yoff_g[:, 128 * q:128 * q + 128] * fs
                 + xp * dexp_ref[:, lane0:lane0 + 128])
            ygs.append(y * z_ref[:, lane0:lane0 + 128].astype(F32))
            state_ref[:, lane0:lane0 + 128] = (
                state_ref[:, lane0:lane0 + 128]
                * cd_scr[F32_ROWS - 1:F32_ROWS, lane0:lane0 + 128] + st_new)
        yg = jnp.concatenate(ygs, axis=1)
        ms = jnp.mean(yg * yg, axis=1, keepdims=True)
        yb = yg * lax.rsqrt(ms + NORM_EPS) * nw_ref[:, g0:g0 + 512]
        u_ref[:, g0:g0 + 512] = yb.astype(BF16)
    xbc_ext[L - BF16_ROWS:L, :] = xbc_ext[2 * L - BF16_ROWS:2 * L, :]


def _mixers(proj, dtp, layer, params, cast_weights, batch, seq, chunks_per_step):
    t = proj.shape[0]
    L = CHUNK
    rows_per_step = L * chunks_per_step
    nc = seq // rows_per_step
    n_steps = batch * nc
    assert len(params) == N_MIXER_CONSTS and len(cast_weights) == N_CAST_WEIGHTS

    slab_in, slab_out, slab_shapes = [], [], []
    for w in cast_weights:
        _, w_rows, w_cols = w.shape
        slab = w_rows // n_steps
        assert slab * n_steps == w_rows and slab % BF16_ROWS == 0
        slab_in.append(pl.BlockSpec((None, slab, w_cols), lambda b, c: (layer, b * nc + c, 0)))
        slab_out.append(pl.BlockSpec((slab, w_cols), lambda b, c: (b * nc + c, 0)))
        slab_shapes.append(jax.ShapeDtypeStruct((w_rows, w_cols), BF16))
    (poolw, pscale, cw, cbias, dtb, alog, dexp, nw, scw, emat, band, shift) = params

    def rows(width, off):
        blk = off // width
        return pl.BlockSpec((rows_per_step, width), lambda b, c: (b * nc + c, blk))

    def lyr(shape):
        nd = len(shape)
        return pl.BlockSpec((None,) + shape, lambda b, c: (layer,) + (0,) * nd)

    in_specs = [
        rows(1024, OFF_PU), rows(1024, OFF_PG), rows(2048, OFF_Z), rows(2048, OFF_X),
        rows(512, OFF_B), rows(512, OFF_C),
        pl.BlockSpec((rows_per_step, DT_PAD), lambda b, c: (b * nc + c, 0)),
        rows(1024, OFF_CB), rows(1024, OFF_CC), rows(1024, OFF_CV), rows(1024, OFF_CG),
        lyr((SSM_GROUPS, POOL_GROUP_DIM, POOL_GROUP_DIM)),
        lyr((1, POOL_WIDTH)),
        lyr((SSM_CONV, SSM_CONV_DIM)),
        lyr((1, SSM_CONV_DIM)),
        lyr((1, DT_PAD)),
        lyr((SSM_HEADS, 1)),
        lyr((1, SSM_INNER)),
        lyr((1, SSM_INNER)),
        lyr((SC_KERNEL, SC_WIDTH)),
        pl.BlockSpec((DT_PAD, SSM_INNER), lambda b, c: (0, 0)),
        pl.BlockSpec((len(POOL_WINDOWS), L, 2 * L), lambda b, c: (0, 0, 0)),
        pl.BlockSpec(((SSM_CONV - 1) * L, 2 * L), lambda b, c: (0, 0)),
    ]
    ucols = SSM_INNER + POOL_WIDTH + SC_WIDTH
    assert len(in_specs) == N_MIXER_ROW_INPUTS + N_MIXER_CONSTS
    outs = pl.pallas_call(
        functools.partial(_mixer_body, chunks_per_step=chunks_per_step),
        grid=(batch, nc),
        in_specs=in_specs + slab_in,
        out_specs=[pl.BlockSpec((rows_per_step, ucols), lambda b, c: (b * nc + c, 0))] + slab_out,
        out_shape=[jax.ShapeDtypeStruct((t, ucols), BF16)] + slab_shapes,
        scratch_shapes=[
            pltpu.VMEM((2 * L, POOL_WIDTH), BF16),
            pltpu.VMEM((2 * L, SSM_CONV_DIM), BF16),
            pltpu.VMEM((CONV_HALO + L, SC_WIDTH), F32),
            pltpu.VMEM((SSM_STATE, SSM_INNER), F32),
            pltpu.VMEM((SSM_GROUPS, SSM_STATE, L), F32),
            pltpu.VMEM((L, SSM_GROUPS * SSM_STATE), BF16),
            pltpu.VMEM((L, DT_PAD), F32),
            pltpu.VMEM((SSM_HEADS, L), F32),
            pltpu.VMEM((SSM_HEADS, L), F32),
            pltpu.VMEM((F32_ROWS, SSM_INNER), F32),
            pltpu.VMEM((L, L), F32),
        ],
        compiler_params=_cparams(("arbitrary", "arbitrary")),
        name="mixers",
    )(proj, proj, proj, proj, proj, proj, dtp, proj, proj, proj, proj,
      poolw, pscale, cw, cbias, dtb, alog, dexp, nw, scw, emat, band, shift, *cast_weights)
    return outs[0], outs[1:]


def _merge_body(ub_ref, ua_ref, uc_ref, wa_ref, wb_ref, wc_ref, g0_ref, g1_ref, g2_ref,
                b0_ref, b1_ref, b2_ref, o_ref):
    def gate(g_ref, b_ref):
        return 0.5 * jnp.tanh(0.5 * (g_ref[...].astype(F32) + b_ref[...])) + 0.5

    ya = jnp.dot(ua_ref[...], wa_ref[...], preferred_element_type=F32)
    yb = jnp.dot(ub_ref[...], wb_ref[...], preferred_element_type=F32)
    yc = jnp.dot(uc_ref[...], wc_ref[...], preferred_element_type=F32)
    acc = gate(g0_ref, b0_ref) * ya + gate(g1_ref, b1_ref) * yb + gate(g2_ref, b2_ref) * yc
    o_ref[...] = acc.astype(o_ref.dtype)


def _merge(u, proj, b_gate, wa_bf, wb_bf, wc_bf, layer, tm, tn):
    t = u.shape[0]
    gate_blk = OFF_GATE // tn
    per_gate = D_MODEL // tn

    def bias(k):
        return pl.BlockSpec((None, 1, tn), lambda i, j: (layer, 0, k * per_gate + i))
    return pl.pallas_call(
        _merge_body,
        grid=(D_MODEL // tn, t // tm),
        in_specs=[
            pl.BlockSpec((tm, SSM_INNER), lambda i, j: (j, 0)),
            pl.BlockSpec((tm, POOL_WIDTH), lambda i, j: (j, SSM_INNER // POOL_WIDTH)),
            pl.BlockSpec((tm, SC_WIDTH), lambda i, j: (j, (SSM_INNER + POOL_WIDTH) // SC_WIDTH)),
            pl.BlockSpec((POOL_WIDTH, tn), lambda i, j: (0, i)),
            pl.BlockSpec((SSM_INNER, tn), lambda i, j: (0, i)),
            pl.BlockSpec((SC_WIDTH, tn), lambda i, j: (0, i)),
            pl.BlockSpec((tm, tn), lambda i, j: (j, gate_blk + i)),
            pl.BlockSpec((tm, tn), lambda i, j: (j, gate_blk + per_gate + i)),
            pl.BlockSpec((tm, tn), lambda i, j: (j, gate_blk + 2 * per_gate + i)),
            bias(0), bias(1), bias(2),
        ],
        out_specs=pl.BlockSpec((tm, tn), lambda i, j: (j, i)),
        out_shape=jax.ShapeDtypeStruct((t, D_MODEL), BF16),
        compiler_params=_cparams(("arbitrary", "arbitrary")),
        name="merge",
    )(u, u, u, wa_bf, wb_bf, wc_bf, proj, proj, proj, b_gate, b_gate, b_gate)


def _out_body(m_ref, w_ref, x_ref, nw_ref, *out_refs, emit_x):
    xn = x_ref[...] + jnp.dot(m_ref[...], w_ref[...], preferred_element_type=F32)
    ms = jnp.mean(xn * xn, axis=-1, keepdims=True)
    hn = xn * lax.rsqrt(ms + NORM_EPS) * nw_ref[...]
    if emit_x:
        out_refs[0][...] = xn
        out_refs[1][...] = hn.astype(out_refs[1].dtype)
    else:
        out_refs[0][...] = hn.astype(out_refs[0].dtype)


def _out_proj(merged, wo_bf, x, nw, nw_layer, last, tm):
    t, d = x.shape
    row_spec = pl.BlockSpec((tm, d), lambda j: (j, 0))
    if last:
        out_shape = [jax.ShapeDtypeStruct((t, d), F32)]
        out_specs = [row_spec]
    else:
        out_shape = [jax.ShapeDtypeStruct((t, d), F32), jax.ShapeDtypeStruct((t, d), BF16)]
        out_specs = [row_spec, row_spec]
    return pl.pallas_call(
        functools.partial(_out_body, emit_x=not last),
        grid=(t // tm,),
        in_specs=[
            row_spec,
            pl.BlockSpec((d, d), lambda j: (0, 0)),
            row_spec,
            pl.BlockSpec((None, 1, d), lambda j: (nw_layer, 0, 0)),
        ],
        out_specs=out_specs,
        out_shape=out_shape,
        compiler_params=_cparams(("arbitrary",)),
        name="out_proj",
    )(merged, wo_bf, x, nw)


def kernel(x, norm_w, w_in, b_gate, pool_w, pool_scale, ssm_conv_w, ssm_conv_b, ssm_dt_bias,
           ssm_a_log, ssm_d, ssm_norm_w, sc_conv_w, w_br_pool, w_br_ssm, w_br_conv, w_out,
           final_norm_w):
    batch, seq, d = x.shape
    depth = w_in.shape[0]
    t = batch * seq
    assert d == D_MODEL and seq % CHUNK == 0
    tm_in = min(2048, t)
    tm_small = min(512, t)
    tn = 1024

    xf = x.reshape(t, d)

    w_in_t = jnp.swapaxes(w_in.astype(F32), 1, 2)
    gate_bias = b_gate.astype(F32)[:, None, :]
    wa = w_br_pool.astype(F32)
    wb = w_br_ssm.astype(F32)
    wc = w_br_conv.astype(F32)
    wo = w_out.astype(F32)

    pad_h = ((0, 0), (0, DT_PAD - SSM_HEADS))
    head_of_lane = jnp.arange(SSM_INNER, dtype=jnp.int32) // SSM_HEAD_DIM
    emat = (jnp.arange(DT_PAD, dtype=jnp.int32)[:, None] == head_of_lane[None, :]).astype(BF16)
    t_idx = jnp.arange(CHUNK, dtype=jnp.int32)[:, None] + CHUNK
    k_idx = jnp.arange(2 * CHUNK, dtype=jnp.int32)[None, :]
    band = jnp.stack([jnp.logical_and(k_idx > t_idx - win, k_idx <= t_idx) for win in POOL_WINDOWS]
                     ).astype(BF16)
    shift = jnp.concatenate([k_idx == t_idx - k for k in range(1, SSM_CONV)], axis=0).astype(BF16)
    mix_params = (
        pool_w.astype(BF16),
        pool_scale.astype(F32)[:, None, :],
        ssm_conv_w.astype(F32),
        ssm_conv_b.astype(F32)[:, None, :],
        jnp.pad(ssm_dt_bias.astype(F32), pad_h)[:, None, :],
        ssm_a_log.astype(F32)[:, :, None],
        jnp.repeat(ssm_d.astype(F32), SSM_HEAD_DIM, axis=1)[:, None, :],
        ssm_norm_w.astype(F32)[:, None, :],
        sc_conv_w.astype(F32),
        emat, band, shift,
    )
    norm_all = jnp.concatenate([norm_w.astype(F32), final_norm_w.astype(F32)[None, :]], axis=0)[:, None, :]

    h = _rmsnorm(xf, norm_all, 0, BF16, tm_small)
    for i in range(depth):
        proj = _in_proj(h, w_in_t, i, tm_in, tn)
        dtp = _dt_proj(h, w_in_t, i, tm_in)
        u, (wa_bf, wb_bf, wc_bf, wo_bf) = _mixers(
            proj, dtp, i, mix_params, (wa, wb, wc, wo), batch, seq,
            next(n for n in (4, 2, 1) if seq % (n * CHUNK) == 0))
        merged = _merge(u, proj, gate_bias, wa_bf, wb_bf, wc_bf, i, tm_small, tn)
        last = i == depth - 1
        outs = _out_proj(merged, wo_bf, xf, norm_all, i + 1, last, tm_small)
        if last:
            return outs[0].reshape(batch, seq, d)
        xf, h = outs
```
